```python
import math
import jax, jax.numpy as jnp
from jax import lax
import numpy as np

D_MODEL = 4096
BATCH = 4
SEQ = 2048
DEPTH = 2
DEC_BATCH = 8
DEC_SEQ = 4
PAST_LEN = 16384
PAGE_SIZE = 128

N_A_LAYERS = DEPTH // 2
N_B_LAYERS = DEPTH - N_A_LAYERS
MEM_TOKENS = 256
MEM_HEADS = 4
MEM_HEAD_DIM = D_MODEL // 16
MEM_WIDTH = MEM_HEADS * MEM_HEAD_DIM
CONV_DIM = D_MODEL - MEM_WIDTH
CONV_WIDTH = 31
DIFF_QK_DIM = D_MODEL // 32
DIFF_V_DIM = 2 * DIFF_QK_DIM
DIFF_HEADS = (D_MODEL - MEM_WIDTH) // DIFF_V_DIM
DIFF_Q_WIDTH = DIFF_HEADS * 2 * DIFF_QK_DIM
DIFF_O_WIDTH = DIFF_HEADS * DIFF_V_DIM
ROT_DIM = DIFF_QK_DIM // 4
ROPE_THETA = 500000.0
D_FF = 256 * ((8 * D_MODEL // 3 + 255) // 256)
FFN_CONV_WIDTH = 3
Q_BLOCK = 128
EPS = 1e-6

kernel_name = 'yoco_conformer_diffattn_decoder_step'


def _rmsnorm(x, g):
    xf = x.astype(jnp.float32)
    y = xf * lax.rsqrt(jnp.mean(xf * xf, axis=-1, keepdims=True) + EPS)
    return (y * g.astype(jnp.float32)).astype(x.dtype)


def _layernorm(x, g, b):
    xf = x.astype(jnp.float32)
    mu = jnp.mean(xf, axis=-1, keepdims=True)
    var = jnp.mean(jnp.square(xf - mu), axis=-1, keepdims=True)
    y = (xf - mu) * lax.rsqrt(var + EPS)
    return (y * g.astype(jnp.float32) + b.astype(jnp.float32)).astype(x.dtype)


def _causal_depthwise(prev, x, w, b):
    xp = jnp.concatenate([prev.astype(x.dtype), x], axis=1)
    y = lax.conv_general_dilated(xp, w[:, None, :].astype(x.dtype), (1,), 'VALID',
                                 dimension_numbers=('NWC', 'WIO', 'NWC'),
                                 feature_group_count=x.shape[-1])
    return y + b.astype(x.dtype), xp[:, xp.shape[1] - (w.shape[0] - 1):]


def _rope(x, pos):
    half = ROT_DIM // 2
    inv = jnp.power(ROPE_THETA, -jnp.arange(0, ROT_DIM, 2, dtype=jnp.float32) / ROT_DIM)
    ang = pos.astype(jnp.float32)[:, None] * inv[None, :]
    cos = jnp.cos(ang)[:, None, None, :]
    sin = jnp.sin(ang)[:, None, None, :]
    xr = x[..., :ROT_DIM].astype(jnp.float32)
    x1, x2 = xr[..., :half], xr[..., half:]
    rot = jnp.concatenate([x1 * cos - x2 * sin, x2 * cos + x1 * sin], axis=-1)
    return jnp.concatenate([rot.astype(x.dtype), x[..., ROT_DIM:]], axis=-1)


def _mem_kv(mem, g_mem, w_mem_kv, g_k):
    B, M = mem.shape[:2]
    kv = _rmsnorm(mem, g_mem) @ w_mem_kv
    k = _rmsnorm(kv[..., :MEM_WIDTH].reshape(B, M, MEM_HEADS, MEM_HEAD_DIM), g_k)
    v = kv[..., MEM_WIDTH:].reshape(B, M, MEM_HEADS, MEM_HEAD_DIM)
    return k, v


def _mem_attend(mq, mem_k, mem_v, g_q):
    B, T = mq.shape[:2]
    q = _rmsnorm(mq.reshape(B, T, MEM_HEADS, MEM_HEAD_DIM), g_q)
    s = jnp.einsum('bqhd,bkhd->bhqk', q, mem_k).astype(jnp.float32) * (MEM_HEAD_DIM ** -0.5)
    p = jax.nn.softmax(s, axis=-1).astype(mem_v.dtype)
    return jnp.einsum('bhqk,bkhd->bqhd', p, mem_v).reshape(B, T, MEM_WIDTH)


def _shared_kv(x, g_kv, w_kv, g_k, pos):
    B, T = x.shape[:2]
    kv = _rmsnorm(x, g_kv) @ w_kv
    k = kv[..., :DIFF_Q_WIDTH].reshape(B, T, DIFF_HEADS, 2, DIFF_QK_DIM)
    k = _rope(_rmsnorm(k, g_k), pos)
    v = kv[..., DIFF_Q_WIDTH:].reshape(B, T, DIFF_HEADS, DIFF_V_DIM)
    return k, v


def _diff_attn_prompt(q, k, v, lam):
    B, S = q.shape[:2]
    nb = S // Q_BLOCK
    qb = jnp.moveaxis(q.reshape(B, nb, Q_BLOCK, DIFF_HEADS, 2, DIFF_QK_DIM), 1, 0)
    k_pos = jnp.arange(S)
    scale = DIFF_QK_DIM ** -0.5

    def block(args):
        i, qi = args
        s = jnp.einsum('bqhcd,bkhcd->bhcqk', qi, k).astype(jnp.float32) * scale
        q_pos = i * Q_BLOCK + jnp.arange(Q_BLOCK)
        s = jnp.where(k_pos[None, :] <= q_pos[:, None], s, -jnp.inf)
        p = jax.nn.softmax(s, axis=-1)
        a = (p[:, :, 0] - lam * p[:, :, 1]).astype(v.dtype)
        return jnp.einsum('bhqk,bkhe->bqhe', a, v)

    o = lax.map(block, (jnp.arange(nb), qb))
    return jnp.moveaxis(o, 0, 1).reshape(B, S, DIFF_HEADS, DIFF_V_DIM)


def _diff_attn_sample(q, k, v, k_past, v_past, lam):
    T = q.shape[1]
    past = k_past.shape[1]
    scale = DIFF_QK_DIM ** -0.5
    s_past = jnp.einsum('bqhcd,bkhcd->bhcqk', q, k_past).astype(jnp.float32) * scale
    s_new = jnp.einsum('bqhcd,bkhcd->bhcqk', q, k).astype(jnp.float32) * scale
    s_new = jnp.where(jnp.tril(jnp.ones((T, T), dtype=bool)), s_new, -jnp.inf)
    p = jax.nn.softmax(jnp.concatenate([s_past, s_new], axis=-1), axis=-1)
    a = (p[:, :, 0] - lam * p[:, :, 1]).astype(v.dtype)
    return (jnp.einsum('bhqk,bkhe->bqhe', a[..., :past], v_past)
            + jnp.einsum('bhqk,bkhe->bqhe', a[..., past:], v))


def setup_inputs(seed: int = 0) -> dict:
    key = jax.random.key(seed)
    ks = iter(jax.random.split(key, 48))
    f32 = jnp.float32

    def nrm(shape, scale=1.0):
        return scale * jax.random.normal(next(ks), shape, f32)

    def gain(shape):
        return 1.0 + nrm(shape, 0.02)

    n_pages = PAST_LEN // PAGE_SIZE
    n_used = DEC_BATCH * n_pages
    n_phys = n_used + max(1, n_used // 4)
    page_table = jax.random.permutation(next(ks), n_phys)[:n_used].reshape(DEC_BATCH, n_pages).astype(jnp.int32)
    d = D_MODEL
    return {
        'x_prompt': nrm((BATCH, SEQ, d)),
        'x_sample': nrm((DEC_BATCH, DEC_SEQ, d)),
        'mem_prompt': nrm((BATCH, MEM_TOKENS, d)),
        'state_conv': nrm((N_A_LAYERS, DEC_BATCH, CONV_WIDTH - 1, CONV_DIM), 0.5),
        'state_ffn': nrm((DEPTH, DEC_BATCH, FFN_CONV_WIDTH - 1, 2 * D_FF)),
        'cache_k': nrm((n_phys, PAGE_SIZE, DIFF_HEADS, 2, DIFF_QK_DIM)),
        'cache_v': nrm((n_phys, PAGE_SIZE, DIFF_HEADS, DIFF_V_DIM)),
        'cache_mem_k': nrm((DEPTH, DEC_BATCH, MEM_TOKENS, MEM_HEADS, MEM_HEAD_DIM)),
        'cache_mem_v': nrm((DEPTH, DEC_BATCH, MEM_TOKENS, MEM_HEADS, MEM_HEAD_DIM)),
        'page_table': page_table,
        'g_mix': gain((DEPTH, d)),
        'g_ffn': gain((DEPTH, d)),
        'w_in_a': nrm((N_A_LAYERS, d, 2 * CONV_DIM + MEM_WIDTH), d ** -0.5),
        'w_dw_a': nrm((N_A_LAYERS, CONV_WIDTH, CONV_DIM), CONV_WIDTH ** -0.5),
        'b_dw_a': nrm((N_A_LAYERS, CONV_DIM), 0.01),
        'ln_g_a': gain((N_A_LAYERS, CONV_DIM)),
        'ln_b_a': nrm((N_A_LAYERS, CONV_DIM), 0.01),
        'w_out_a': nrm((N_A_LAYERS, CONV_DIM + MEM_WIDTH, d), (CONV_DIM + MEM_WIDTH) ** -0.5),
        'g_kv': gain((d,)),
        'w_kv': nrm((d, DIFF_Q_WIDTH + DIFF_O_WIDTH), d ** -0.5),
        'g_k_diff': gain((DIFF_QK_DIM,)),
        'w_in_b': nrm((N_B_LAYERS, d, DIFF_Q_WIDTH + MEM_WIDTH), d ** -0.5),
        'g_q_diff': gain((N_B_LAYERS, DIFF_QK_DIM)),
        'lambda_q1': nrm((N_B_LAYERS, DIFF_QK_DIM), 0.1),
        'lambda_k1': nrm((N_B_LAYERS, DIFF_QK_DIM), 0.1),
        'lambda_q2': nrm((N_B_LAYERS, DIFF_QK_DIM), 0.1),
        'lambda_k2': nrm((N_B_LAYERS, DIFF_QK_DIM), 0.1),
        'g_subln': gain((N_B_LAYERS, DIFF_V_DIM)),
        'w_out_b': nrm((N_B_LAYERS, DIFF_O_WIDTH + MEM_WIDTH, d), (DIFF_O_WIDTH + MEM_WIDTH) ** -0.5),
        'g_mem': gain((DEPTH, d)),
        'w_mem_kv': nrm((DEPTH, d, 2 * MEM_WIDTH), d ** -0.5),
        'g_mem_q': gain((DEPTH, MEM_HEAD_DIM)),
        'g_mem_k': gain((DEPTH, MEM_HEAD_DIM)),
        'w_ffn_up': nrm((DEPTH, d, 2 * D_FF), d ** -0.5),
        'w_ffn_dw': nrm((DEPTH, FFN_CONV_WIDTH, 2 * D_FF), FFN_CONV_WIDTH ** -0.5),
        'b_ffn_dw': nrm((DEPTH, 2 * D_FF), 0.01),
        'w_ffn_down': nrm((DEPTH, D_FF, d), D_FF ** -0.5),
    }


def reference(x_prompt, x_sample, mem_prompt, state_conv, state_ffn, cache_k, cache_v,
              cache_mem_k, cache_mem_v, page_table, g_mix, g_ffn, w_in_a, w_dw_a, b_dw_a,
              ln_g_a, ln_b_a, w_out_a, g_kv, w_kv, g_k_diff, w_in_b, g_q_diff, lambda_q1,
              lambda_k1, lambda_q2, lambda_k2, g_subln, w_out_b, g_mem, w_mem_kv, g_mem_q,
              g_mem_k, w_ffn_up, w_ffn_dw, b_ffn_dw, w_ffn_down):

    def trunk(x, pos, mem_k, mem_v, conv_prev, ffn_prev, attend):
        B, T = x.shape[:2]
        conv_new, ffn_new = [], []
        k = v = None
        for l in range(DEPTH):
            h = _rmsnorm(x, g_mix[l])
            if l < N_A_LAYERS:
                i = l
                u = h @ w_in_a[i]
                glu = u[..., :CONV_DIM] * jax.nn.sigmoid(u[..., CONV_DIM:2 * CONV_DIM])
                c, st = _causal_depthwise(conv_prev[i], glu, w_dw_a[i], b_dw_a[i])
                conv_new.append(st)
                c = jax.nn.silu(_layernorm(c, ln_g_a[i], ln_b_a[i]))
                m = _mem_attend(u[..., 2 * CONV_DIM:], mem_k[l], mem_v[l], g_mem_q[l])
                x = x + jnp.concatenate([c, m], axis=-1) @ w_out_a[i]
            else:
                j = l - N_A_LAYERS
                lambda_init = 0.8 - 0.6 * math.exp(-0.3 * l)
                u = h @ w_in_b[j]
                q = u[..., :DIFF_Q_WIDTH].reshape(B, T, DIFF_HEADS, 2, DIFF_QK_DIM)
                q = _rope(_rmsnorm(q, g_q_diff[j]), pos)
                lam = (jnp.exp(jnp.sum(lambda_q1[j].astype(jnp.float32) * lambda_k1[j].astype(jnp.float32)))
                       - jnp.exp(jnp.sum(lambda_q2[j].astype(jnp.float32) * lambda_k2[j].astype(jnp.float32)))
                       + lambda_init)
                o = attend(q, k, v, lam)
                o = _rmsnorm(o, g_subln[j]) * (1.0 - lambda_init)
                m = _mem_attend(u[..., DIFF_Q_WIDTH:], mem_k[l], mem_v[l], g_mem_q[l])
                x = x + jnp.concatenate([o.reshape(B, T, DIFF_O_WIDTH), m], axis=-1) @ w_out_b[j]
            h = _rmsnorm(x, g_ffn[l])
            up, st = _causal_depthwise(ffn_prev[l], h @ w_ffn_up[l], w_ffn_dw[l], b_ffn_dw[l])
            ffn_new.append(st)
            x = x + (jax.nn.silu(up[..., :D_FF]) * up[..., D_FF:]) @ w_ffn_down[l]
            if l == N_A_LAYERS - 1:
                k, v = _shared_kv(x, g_kv, w_kv, g_k_diff, pos)
        return x, jnp.stack(conv_new), jnp.stack(ffn_new), k, v

    bp, sp = x_prompt.shape[:2]
    pos_p = jnp.arange(sp)
    mem_kv = [_mem_kv(mem_prompt, g_mem[l], w_mem_kv[l], g_mem_k[l]) for l in range(DEPTH)]
    mem_k_prompt = jnp.stack([kv[0] for kv in mem_kv])
    mem_v_prompt = jnp.stack([kv[1] for kv in mem_kv])
    conv0 = jnp.zeros((N_A_LAYERS, bp, CONV_WIDTH - 1, CONV_DIM), x_prompt.dtype)
    ffn0 = jnp.zeros((DEPTH, bp, FFN_CONV_WIDTH - 1, 2 * D_FF), x_prompt.dtype)
    y_prompt, conv_prompt, ffn_prompt, k_prompt, v_prompt = trunk(
        x_prompt, pos_p, mem_k_prompt, mem_v_prompt, conv0, ffn0, _diff_attn_prompt)

    bs, ts = x_sample.shape[:2]
    past_len = page_table.shape[1] * PAGE_SIZE
    pos_s = past_len + jnp.arange(ts)
    k_past = cache_k[page_table].reshape(bs, past_len, DIFF_HEADS, 2, DIFF_QK_DIM)
    v_past = cache_v[page_table].reshape(bs, past_len, DIFF_HEADS, DIFF_V_DIM)

    def attend_sample(q, k, v, lam):
        return _diff_attn_sample(q, k, v, k_past, v_past, lam)

    y_sample, conv_sample, ffn_sample, k_sample, v_sample = trunk(
        x_sample, pos_s, cache_mem_k, cache_mem_v, state_conv, state_ffn, attend_sample)

    return (y_prompt, y_sample, conv_prompt, ffn_prompt, k_prompt, v_prompt, mem_k_prompt, mem_v_prompt,
            conv_sample, ffn_sample, k_sample, v_sample)
```

```python
import functools
import math

import jax
import jax.numpy as jnp
from jax import lax
from jax.experimental import pallas as pl
from jax.experimental.pallas import tpu as pltpu

EPS = 1e-6
ROPE_THETA = 500000.0
MEM_HEADS = 4
CONV_WIDTH = 31
FFN_CONV_WIDTH = 3
DIFF_QK_DIM = 128
DIFF_V_DIM = 256
ROT_DIM = DIFF_QK_DIM // 4
PAGE_SIZE = 128

LANES = 128
SUBLANES_F32 = 8
SUBLANES_BF16 = 16
VMEM_LIMIT_BYTES = 56 * 1024 * 1024
MATMUL_VMEM_BUDGET = 44 * 1024 * 1024

CONV_HALO = 32
FFN_HALO = 8
SAMPLE_T_PAD = 16
SAMPLE_PAGES_PER_STEP = 4
SAMPLE_COLS = 128

BF16 = jnp.bfloat16
F32 = jnp.float32


def _params(*sem):
    return pltpu.CompilerParams(dimension_semantics=sem, vmem_limit_bytes=VMEM_LIMIT_BYTES)


def _rmsnorm_cast_body(x_ref, g_ref, o_ref):
    x = x_ref[...]
    y = x * lax.rsqrt(jnp.mean(x * x, axis=-1, keepdims=True) + EPS)
    o_ref[...] = (y * g_ref[...]).astype(o_ref.dtype)


def rmsnorm_cast(x, g):
    m, d = x.shape
    tm = min(m, 256)
    assert m % tm == 0
    return pl.pallas_call(
        _rmsnorm_cast_body,
        grid=(m // tm,),
        in_specs=[pl.BlockSpec((tm, d), lambda i: (i, 0)),
                  pl.BlockSpec((1, d), lambda i: (0, 0))],
        out_specs=pl.BlockSpec((tm, d), lambda i: (i, 0)),
        out_shape=jax.ShapeDtypeStruct((m, d), BF16),
        compiler_params=_params("parallel"),
        name="rmsnorm_cast",
    )(x, g.reshape(1, d))


def _matmul_body(a_ref, w_ref, *rest, has_res):
    acc = jnp.dot(a_ref[...], w_ref[...], preferred_element_type=F32)
    if has_res:
        res_ref, o_ref = rest
        acc = res_ref[...] + acc
    else:
        (o_ref,) = rest
    o_ref[...] = acc.astype(o_ref.dtype)


def _matmul_tiles(m, k, n, has_res):
    for tm in (1024, 512, 256, 128, m):
        if m % tm or tm % SUBLANES_BF16:
            continue
        for tn in (1024, 512, 256, 128):
            if n % tn:
                continue
            est = 2 * tm * k * 2 + 2 * k * tn * 2 + 2 * tm * tn * 4 * (2 if has_res else 1)
            if est <= MATMUL_VMEM_BUDGET:
                return tm, tn
    raise ValueError(f"no matmul tiling for {(m, k, n)}")


def matmul(a, w, res=None):
    m, k = a.shape
    n = w.shape[1]
    has_res = res is not None
    tm, tn = _matmul_tiles(m, k, n, has_res)
    in_specs = [pl.BlockSpec((tm, k), lambda i, j: (i, 0)),
                pl.BlockSpec((k, tn), lambda i, j: (0, j))]
    args = [a, w]
    if has_res:
        in_specs.append(pl.BlockSpec((tm, tn), lambda i, j: (i, j)))
        args.append(res)
    return pl.pallas_call(
        functools.partial(_matmul_body, has_res=has_res),
        grid=(m // tm, n // tn),
        in_specs=in_specs,
        out_specs=pl.BlockSpec((tm, tn), lambda i, j: (i, j)),
        out_shape=jax.ShapeDtypeStruct((m, n), F32),
        compiler_params=_params("parallel", "parallel"),
        name="matmul_res" if has_res else "matmul",
    )(*args)


def _ffn_up_body(a_ref, wg_ref, wu_ref, dwg_ref, dwu_ref, bg_ref, bu_ref, pg_ref, pu_ref,
                 o_ref, sg_ref, su_ref, bufg, bufu, *, tm, tiles_per_seq, t_valid_last):
    i = pl.program_id(1)

    @pl.when(i % tiles_per_seq == 0)
    def _():
        bufg[0:FFN_HALO, :] = jnp.zeros((FFN_HALO, bufg.shape[1]), F32)
        bufu[0:FFN_HALO, :] = jnp.zeros((FFN_HALO, bufu.shape[1]), F32)
        bufg[FFN_HALO - 2:FFN_HALO, :] = pg_ref[0]
        bufu[FFN_HALO - 2:FFN_HALO, :] = pu_ref[0]

    a = a_ref[...]

    def half(w_ref, dw_ref, b_ref, buf, s_ref):
        u = jnp.dot(a, w_ref[...], preferred_element_type=F32)
        buf[FFN_HALO:FFN_HALO + tm, :] = u
        y = (dw_ref[0:1, :] * buf[FFN_HALO - 2:FFN_HALO - 2 + tm, :]
             + dw_ref[1:2, :] * buf[FFN_HALO - 1:FFN_HALO - 1 + tm, :]
             + dw_ref[2:3, :] * u) + b_ref[...]
        s_ref[0] = buf[FFN_HALO + t_valid_last - 2:FFN_HALO + t_valid_last, :]
        buf[0:FFN_HALO, :] = buf[tm:tm + FFN_HALO, :]
        return y

    yg = half(wg_ref, dwg_ref, bg_ref, bufg, sg_ref)
    yu = half(wu_ref, dwu_ref, bu_ref, bufu, su_ref)
    o_ref[...] = (yg * jax.nn.sigmoid(yg) * yu).astype(o_ref.dtype)


def ffn_up(a, w_up, w_dw, b_dw, prev, seq_len, t_valid):
    m, k = a.shape
    f2 = w_up.shape[1]
    f = f2 // 2
    tn = 256
    assert f % tn == 0
    nj = f // tn
    tm = min(seq_len, 1024)
    assert seq_len % tm == 0
    tiles_per_seq = seq_len // tm
    nb = m // seq_len
    t_valid_last = t_valid - (tiles_per_seq - 1) * tm
    assert 2 <= t_valid_last <= tm
    b2 = b_dw.reshape(1, f2)
    body = functools.partial(_ffn_up_body, tm=tm, tiles_per_seq=tiles_per_seq,
                             t_valid_last=t_valid_last)
    g, sg, su = pl.pallas_call(
        body,
        grid=(nj, m // tm),
        in_specs=[
            pl.BlockSpec((tm, k), lambda j, i: (i, 0)),
            pl.BlockSpec((k, tn), lambda j, i: (0, j)),
            pl.BlockSpec((k, tn), lambda j, i: (0, j + nj)),
            pl.BlockSpec((FFN_CONV_WIDTH, tn), lambda j, i: (0, j)),
            pl.BlockSpec((FFN_CONV_WIDTH, tn), lambda j, i: (0, j + nj)),
            pl.BlockSpec((1, tn), lambda j, i: (0, j)),
            pl.BlockSpec((1, tn), lambda j, i: (0, j + nj)),
            pl.BlockSpec((1, 2, tn), lambda j, i: (i // tiles_per_seq, 0, j)),
            pl.BlockSpec((1, 2, tn), lambda j, i: (i // tiles_per_seq, 0, j + nj)),
        ],
        out_specs=[
            pl.BlockSpec((tm, tn), lambda j, i: (i, j)),
            pl.BlockSpec((1, 2, tn), lambda j, i: (i // tiles_per_seq, 0, j)),
            pl.BlockSpec((1, 2, tn), lambda j, i: (i // tiles_per_seq, 0, j)),
        ],
        out_shape=[jax.ShapeDtypeStruct((m, f), BF16),
                   jax.ShapeDtypeStruct((nb, 2, f), F32),
                   jax.ShapeDtypeStruct((nb, 2, f), F32)],
        scratch_shapes=[pltpu.VMEM((tm + FFN_HALO, tn), F32),
                        pltpu.VMEM((tm + FFN_HALO, tn), F32)],
        compiler_params=_params("parallel", "arbitrary"),
        name="ffn_up",
    )(a, w_up, w_up, w_dw, w_dw, b2, b2, prev, prev)
    return g, jnp.concatenate([sg, su], axis=-1)


def _conv_module_body(a_ref, gate_ref, prev_ref, w_ref, b_ref, lg_ref, lb_ref,
                      o_ref, st_ref, buf, *, tt, tiles_per_seq, t_valid_last):
    i = pl.program_id(1)

    @pl.when(i % tiles_per_seq == 0)
    def _():
        buf[0:CONV_HALO, :] = prev_ref[0]

    glu = a_ref[0] * jax.nn.sigmoid(gate_ref[0])
    buf[CONV_HALO:CONV_HALO + tt, :] = glu
    base = CONV_HALO - (CONV_WIDTH - 1)
    acc = w_ref[0:1, :] * buf[base:base + tt, :]
    for j in range(1, CONV_WIDTH):
        acc = acc + w_ref[j:j + 1, :] * buf[base + j:base + j + tt, :]
    c = acc + b_ref[...]
    mu = jnp.mean(c, axis=-1, keepdims=True)
    d = c - mu
    var = jnp.mean(d * d, axis=-1, keepdims=True)
    y = d * lax.rsqrt(var + EPS) * lg_ref[...] + lb_ref[...]
    o_ref[0] = (y * jax.nn.sigmoid(y)).astype(o_ref.dtype)
    st_ref[0] = buf[base + t_valid_last:base + t_valid_last + CONV_WIDTH - 1, :]
    buf[0:CONV_HALO, :] = buf[tt:tt + CONV_HALO, :]


def conv_module(u, prev, w_dw, b_dw, ln_g, ln_b, c_dim, t_valid):
    nb, t, _ = u.shape
    tt = min(t, 256)
    assert t % tt == 0
    tiles_per_seq = t // tt
    t_valid_last = t_valid - (tiles_per_seq - 1) * tt
    assert 0 < t_valid_last <= tt
    prev_pad = jnp.pad(prev, ((0, 0), (CONV_HALO - (CONV_WIDTH - 1), 0), (0, 0)))
    body = functools.partial(_conv_module_body, tt=tt, tiles_per_seq=tiles_per_seq,
                             t_valid_last=t_valid_last)
    row = lambda v: v.reshape(1, c_dim)
    return pl.pallas_call(
        body,
        grid=(nb, tiles_per_seq),
        in_specs=[
            pl.BlockSpec((1, tt, c_dim), lambda b, i: (b, i, 0)),
            pl.BlockSpec((1, tt, c_dim), lambda b, i: (b, i, 1)),
            pl.BlockSpec((1, CONV_HALO, c_dim), lambda b, i: (b, 0, 0)),
            pl.BlockSpec((CONV_WIDTH, c_dim), lambda b, i: (0, 0)),
            pl.BlockSpec((1, c_dim), lambda b, i: (0, 0)),
            pl.BlockSpec((1, c_dim), lambda b, i: (0, 0)),
            pl.BlockSpec((1, c_dim), lambda b, i: (0, 0)),
        ],
        out_specs=[
            pl.BlockSpec((1, tt, c_dim), lambda b, i: (b, i, 0)),
            pl.BlockSpec((1, CONV_WIDTH - 1, c_dim), lambda b, i: (b, 0, 0)),
        ],
        out_shape=[jax.ShapeDtypeStruct((nb, t, c_dim), BF16),
                   jax.ShapeDtypeStruct((nb, CONV_WIDTH - 1, c_dim), F32)],
        scratch_shapes=[pltpu.VMEM((tt + CONV_HALO, c_dim), F32)],
        compiler_params=_params("parallel", "arbitrary"),
        name="conv_module",
    )(u, u, prev_pad, w_dw, row(b_dw), row(ln_g), row(ln_b))


def _mem_attend_body(q_ref, k_ref, v_ref, g_ref, o_ref, *, head_dim):
    scale = head_dim ** -0.5
    for h in range(MEM_HEADS):
        sl = slice(h * head_dim, (h + 1) * head_dim)
        q = q_ref[0, :, sl]
        q = q * lax.rsqrt(jnp.mean(q * q, axis=-1, keepdims=True) + EPS) * g_ref[...]
        k = k_ref[0, :, sl].astype(BF16)
        s = lax.dot_general(q.astype(BF16), k, (((1,), (1,)), ((), ())),
                            preferred_element_type=F32) * scale
        s = s - jnp.max(s, axis=-1, keepdims=True)
        e = jnp.exp(s)
        p = e / jnp.sum(e, axis=-1, keepdims=True)
        o = jnp.dot(p.astype(BF16), v_ref[0, :, sl].astype(BF16), preferred_element_type=F32)
        o_ref[0, :, sl] = o.astype(o_ref.dtype)


def mem_attend(u, col_block, mem_k, mem_v, g_q):
    nb, t, _ = u.shape
    mt, w = mem_k.shape[1:]
    head_dim = w // MEM_HEADS
    tq = min(t, 512)
    assert t % tq == 0
    return pl.pallas_call(
        functools.partial(_mem_attend_body, head_dim=head_dim),
        grid=(nb, t // tq),
        in_specs=[
            pl.BlockSpec((1, tq, w), lambda b, i: (b, i, col_block)),
            pl.BlockSpec((1, mt, w), lambda b, i: (b, 0, 0)),
            pl.BlockSpec((1, mt, w), lambda b, i: (b, 0, 0)),
            pl.BlockSpec((1, head_dim), lambda b, i: (0, 0)),
        ],
        out_specs=pl.BlockSpec((1, tq, w), lambda b, i: (b, i, 0)),
        out_shape=jax.ShapeDtypeStruct((nb, t, w), BF16),
        compiler_params=_params("parallel", "parallel"),
        name="mem_attend",
    )(u, mem_k, mem_v, g_q.reshape(1, head_dim))


def _group_norm_body(x_ref, g_ref, o_ref, *, group):
    for h in range(x_ref.shape[1] // group):
        sl = slice(h * group, (h + 1) * group)
        x = x_ref[:, sl]
        y = x * lax.rsqrt(jnp.mean(x * x, axis=-1, keepdims=True) + EPS)
        o_ref[:, sl] = (y * g_ref[...]).astype(o_ref.dtype)


def group_rmsnorm(x, width, group, g):
    m = x.shape[0]
    tm = min(m, 512)
    assert m % tm == 0
    return pl.pallas_call(
        functools.partial(_group_norm_body, group=group),
        grid=(m // tm,),
        in_specs=[pl.BlockSpec((tm, width), lambda i: (i, 0)),
                  pl.BlockSpec((1, group), lambda i: (0, 0))],
        out_specs=pl.BlockSpec((tm, width), lambda i: (i, 0)),
        out_shape=jax.ShapeDtypeStruct((m, width), F32),
        compiler_params=_params("parallel"),
        name="group_rmsnorm",
    )(x, g.reshape(1, group))


def _norm_rope_body(x_ref, g_ref, cos_ref, sa_ref, sb_ref, *o_refs):
    half = ROT_DIM // 2
    cos, sa, sb = cos_ref[...], sa_ref[...], sb_ref[...]
    for h in range(x_ref.shape[2] // DIFF_QK_DIM):
        sl = slice(h * DIFF_QK_DIM, (h + 1) * DIFF_QK_DIM)
        x = x_ref[0, :, sl]
        y = x * lax.rsqrt(jnp.mean(x * x, axis=-1, keepdims=True) + EPS) * g_ref[...]
        r = (y * cos + pltpu.roll(y, DIFF_QK_DIM - half, axis=1) * sa
             + pltpu.roll(y, half, axis=1) * sb)
        for o_ref in o_refs:
            o_ref[0, :, sl] = r.astype(o_ref.dtype)


def rope_tables(pos):
    half = ROT_DIM // 2
    inv = jnp.power(ROPE_THETA, -jnp.arange(0, ROT_DIM, 2, dtype=F32) / ROT_DIM)
    ang = pos.astype(F32)[:, None] * inv[None, :]
    cos, sin = jnp.cos(ang), jnp.sin(ang)
    t = pos.shape[0]
    rest = DIFF_QK_DIM - ROT_DIM
    cos_t = jnp.concatenate([cos, cos, jnp.ones((t, rest), F32)], axis=1)
    sa_t = jnp.concatenate([-sin, jnp.zeros((t, half + rest), F32)], axis=1)
    sb_t = jnp.concatenate([jnp.zeros((t, half), F32), sin, jnp.zeros((t, rest), F32)], axis=1)
    return cos_t, sa_t, sb_t


def norm_rope(x, width, g, tables, out_dtypes):
    nb, t, _ = x.shape
    tt = min(t, 256)
    assert t % tt == 0
    tab_spec = pl.BlockSpec((tt, DIFF_QK_DIM), lambda b, i: (i, 0))
    return pl.pallas_call(
        _norm_rope_body,
        grid=(nb, t // tt),
        in_specs=[pl.BlockSpec((1, tt, width), lambda b, i: (b, i, 0)),
                  pl.BlockSpec((1, DIFF_QK_DIM), lambda b, i: (0, 0)),
                  tab_spec, tab_spec, tab_spec],
        out_specs=[pl.BlockSpec((1, tt, width), lambda b, i: (b, i, 0)) for _ in out_dtypes],
        out_shape=[jax.ShapeDtypeStruct((nb, t, width), dt) for dt in out_dtypes],
        compiler_params=_params("parallel", "parallel"),
        name="norm_rope",
    )(x, g.reshape(1, DIFF_QK_DIM), *tables)


def _subln(o, g, post_scale):
    y = o * lax.rsqrt(jnp.mean(o * o, axis=-1, keepdims=True) + EPS)
    return y * g * post_scale


def _diff_attn_prompt_body(lam_ref, q_ref, k_ref, v_ref, g_ref, o_ref, m_sc, l_sc, acc_sc,
                           *, tq, post_scale):
    qi = pl.program_id(2)
    scale = DIFF_QK_DIM ** -0.5
    m_sc[...] = jnp.full(m_sc.shape, -jnp.inf, F32)
    l_sc[...] = jnp.zeros(l_sc.shape, F32)
    acc_sc[...] = jnp.zeros(acc_sc.shape, F32)

    def step(ki, masked):
        start = pl.multiple_of(ki * tq, tq)
        v = v_ref[0, pl.ds(start, tq), :]
        for c in range(2):
            sl = slice(c * DIFF_QK_DIM, (c + 1) * DIFF_QK_DIM)
            s = lax.dot_general(q_ref[0, :, sl], k_ref[0, pl.ds(start, tq), sl],
                                (((1,), (1,)), ((), ())), preferred_element_type=F32) * scale
            if masked:
                row = lax.broadcasted_iota(jnp.int32, s.shape, 0)
                col = lax.broadcasted_iota(jnp.int32, s.shape, 1)
                s = jnp.where(col <= row, s, -jnp.inf)
            m_prev = m_sc[c]
            m_new = jnp.maximum(m_prev, jnp.max(s, axis=-1, keepdims=True))
            alpha = jnp.exp(m_prev - m_new)
            p = jnp.exp(s - m_new)
            l_sc[c] = alpha * l_sc[c] + jnp.sum(p, axis=-1, keepdims=True)
            acc_sc[c] = alpha * acc_sc[c] + jnp.dot(p.astype(BF16), v, preferred_element_type=F32)
            m_sc[c] = m_new

    def full_step(ki, carry):
        step(ki, False)
        return carry

    lax.fori_loop(0, qi, full_step, 0)
    step(qi, True)
    o = acc_sc[0] / l_sc[0] - lam_ref[0, 0] * (acc_sc[1] / l_sc[1])
    o_ref[0] = _subln(o, g_ref[...], post_scale).astype(o_ref.dtype)


def diff_attn_prompt(q, k, v, lam, g_subln, post_scale):
    nb, t, w = q.shape
    nh = w // DIFF_V_DIM
    tq = min(t, 512)
    assert t % tq == 0
    body = functools.partial(_diff_attn_prompt_body, tq=tq, post_scale=post_scale)
    return pl.pallas_call(
        body,
        grid=(nb, nh, t // tq),
        in_specs=[
            pl.BlockSpec(memory_space=pltpu.SMEM),
            pl.BlockSpec((1, tq, DIFF_V_DIM), lambda b, h, i: (b, i, h)),
            pl.BlockSpec((1, t, DIFF_V_DIM), lambda b, h, i: (b, 0, h)),
            pl.BlockSpec((1, t, DIFF_V_DIM), lambda b, h, i: (b, 0, h)),
            pl.BlockSpec((1, DIFF_V_DIM), lambda b, h, i: (0, 0)),
        ],
        out_specs=pl.BlockSpec((1, tq, DIFF_V_DIM), lambda b, h, i: (b, i, h)),
        out_shape=jax.ShapeDtypeStruct((nb, t, w), BF16),
        scratch_shapes=[pltpu.VMEM((2, tq, 1), F32), pltpu.VMEM((2, tq, 1), F32),
                        pltpu.VMEM((2, tq, DIFF_V_DIM), F32)],
        compiler_params=_params("parallel", "parallel", "parallel"),
        name="diff_attn_prompt",
    )(lam.reshape(1, 1), q, k, v, g_subln.reshape(1, DIFF_V_DIM))


def _diff_attn_sample_body(pt_ref, lam_ref, qt_ref, *rest, n_steps, pages_per_step, n_groups,
                           post_scale):
    pp = pages_per_step
    k_refs, v_refs = rest[:pp], rest[pp:2 * pp]
    kn_ref, vn_ref, bias_ref, g_ref, o_ref, m_sc, l_sc, acc_sc = rest[2 * pp:]
    p_id = pl.program_id(1)
    scale = DIFF_QK_DIM ** -0.5

    @pl.when(p_id == 0)
    def _():
        m_sc[...] = jnp.full(m_sc.shape, -jnp.inf, F32)
        l_sc[...] = jnp.zeros(l_sc.shape, F32)
        acc_sc[...] = jnp.zeros(acc_sc.shape, F32)

    def update(ks, vs, bias):
        qt = qt_ref[0]
        s = [lax.dot_general(qt, kk.astype(BF16), (((1,), (1,)), ((), ())),
                             preferred_element_type=F32) * scale for kk in ks]
        if bias is not None:
            s = [x + bias for x in s]
        m_prev = m_sc[...]
        m_new = m_prev
        for x in s:
            m_new = jnp.maximum(m_new, jnp.max(x, axis=-1, keepdims=True))
        alpha = jnp.exp(m_prev - m_new)
        l_new = alpha * l_sc[...]
        pv = None
        for x, vv in zip(s, vs):
            p = jnp.exp(x - m_new)
            l_new = l_new + jnp.sum(p, axis=-1, keepdims=True)
            d = jnp.dot(p.astype(BF16), vv.astype(BF16), preferred_element_type=F32)
            pv = d if pv is None else pv + d
        acc_sc[...] = alpha * acc_sc[...] + pv
        l_sc[...] = l_new
        m_sc[...] = m_new

    @pl.when(p_id < n_steps)
    def _():
        update([r[0] for r in k_refs], [r[0] for r in v_refs], None)

    @pl.when(p_id == n_steps)
    def _():
        update([kn_ref[0]], [vn_ref[0]], bias_ref[...])
        lam = lam_ref[0, 0]
        half = n_groups * SUBLANES_F32
        for g in range(n_groups):
            lanes = slice(g * 2 * DIFF_V_DIM, (g + 1) * 2 * DIFF_V_DIM)
            r0 = slice(g * SUBLANES_F32, (g + 1) * SUBLANES_F32)
            r1 = slice(half + g * SUBLANES_F32, half + (g + 1) * SUBLANES_F32)
            o = acc_sc[r0, lanes] / l_sc[r0, :] - lam * (acc_sc[r1, lanes] / l_sc[r1, :])
            for e in range(2):
                el = slice(e * DIFF_V_DIM, (e + 1) * DIFF_V_DIM)
                out_l = slice(g * 2 * DIFF_V_DIM + e * DIFF_V_DIM, g * 2 * DIFF_V_DIM + (e + 1) * DIFF_V_DIM)
                o_ref[0, :, out_l] = _subln(o[:, el], g_ref[...], post_scale).astype(o_ref.dtype)


def diff_attn_sample(q, k_new, v_new, cache_k, cache_v, page_table, lam, g_subln, post_scale):
    nb, t, nh = q.shape[:3]
    w = nh * DIFF_V_DIM
    n_pages = page_table.shape[1]
    pp = SAMPLE_PAGES_PER_STEP if n_pages % SAMPLE_PAGES_PER_STEP == 0 else 1
    n_steps = n_pages // pp
    assert nh % 2 == 0 and 2 * t == SUBLANES_F32 and nh * 2 * t <= SAMPLE_COLS
    n_groups = nh // 2
    qf = q.astype(F32).reshape(nb, t, nh, 2, DIFF_QK_DIM)
    eye = jnp.eye(nh * 2, dtype=F32).reshape(nh, 2, nh, 2)
    qt = jnp.einsum("bqhcd,hcxy->bchqxyd", qf, eye).reshape(nb, 2 * nh * t, nh * 2 * DIFF_QK_DIM)
    qt = jnp.pad(qt, ((0, 0), (0, SAMPLE_COLS - 2 * nh * t), (0, 0))).astype(BF16)
    kn = jnp.pad(k_new.astype(F32).reshape(nb, t, w), ((0, 0), (0, PAGE_SIZE - t), (0, 0)))
    vn = jnp.pad(v_new.astype(F32).reshape(nb, t, w), ((0, 0), (0, PAGE_SIZE - t), (0, 0)))
    col_q = jnp.arange(SAMPLE_COLS) % t
    tok = jnp.arange(PAGE_SIZE)
    bias = jnp.where((tok[None, :] <= col_q[:, None]) & (tok[None, :] < t), 0.0, -jnp.inf).astype(F32)
    ck = cache_k.reshape(cache_k.shape[0], PAGE_SIZE, w)
    cv = cache_v.reshape(cache_v.shape[0], PAGE_SIZE, w)

    def page_spec(r):
        return pl.BlockSpec(
            (1, PAGE_SIZE, w),
            lambda b, p, pt: (pt[b, jnp.minimum(p, n_steps - 1) * pp + r], 0, 0))

    body = functools.partial(_diff_attn_sample_body, n_steps=n_steps, pages_per_step=pp,
                             n_groups=n_groups, post_scale=post_scale)
    grid_spec = pltpu.PrefetchScalarGridSpec(
        num_scalar_prefetch=1,
        grid=(nb, n_steps + 1),
        in_specs=(
            [pl.BlockSpec(memory_space=pltpu.SMEM),
             pl.BlockSpec((1, SAMPLE_COLS, w), lambda b, p, pt: (b, 0, 0))]
            + [page_spec(r) for r in range(pp)] + [page_spec(r) for r in range(pp)]
            + [pl.BlockSpec((1, PAGE_SIZE, w), lambda b, p, pt: (b, 0, 0)),
               pl.BlockSpec((1, PAGE_SIZE, w), lambda b, p, pt: (b, 0, 0)),
               pl.BlockSpec((SAMPLE_COLS, PAGE_SIZE), lambda b, p, pt: (0, 0)),
               pl.BlockSpec((1, DIFF_V_DIM), lambda b, p, pt: (0, 0))]),
        out_specs=pl.BlockSpec((1, SUBLANES_F32, w), lambda b, p, pt: (b, 0, 0)),
        scratch_shapes=[pltpu.VMEM((SAMPLE_COLS, 1), F32), pltpu.VMEM((SAMPLE_COLS, 1), F32),
                        pltpu.VMEM((SAMPLE_COLS, w), F32)],
    )
    out = pl.pallas_call(
        body,
        grid_spec=grid_spec,
        out_shape=jax.ShapeDtypeStruct((nb, SUBLANES_F32, w), BF16),
        compiler_params=_params("parallel", "arbitrary"),
        name="diff_attn_sample",
    )(page_table, lam.reshape(1, 1), qt, *([ck] * pp), *([cv] * pp), kn, vn, bias,
      g_subln.reshape(1, DIFF_V_DIM))
    out = out.reshape(nb, 2, t, n_groups, 2, DIFF_V_DIM)
    out = jnp.stack([out[:, 0, :, :, 0], out[:, 1, :, :, 1]], axis=3)
    return out.reshape(nb, t, w)


def kernel(x_prompt, x_sample, mem_prompt, state_conv, state_ffn, cache_k, cache_v, cache_mem_k, cache_mem_v, page_table, g_mix, g_ffn, w_in_a, w_dw_a, b_dw_a, ln_g_a, ln_b_a, w_out_a, g_kv, w_kv, g_k_diff, w_in_b, g_q_diff, lambda_q1, lambda_k1, lambda_q2, lambda_k2, g_subln, w_out_b, g_mem, w_mem_kv, g_mem_q, g_mem_k, w_ffn_up, w_ffn_dw, b_ffn_dw, w_ffn_down):
    depth = g_mix.shape[0]
    n_a = w_in_a.shape[0]
    d_model = x_prompt.shape[-1]
    conv_dim = w_dw_a.shape[-1]
    mem_width = w_mem_kv.shape[-1] // 2
    mem_head_dim = mem_width // MEM_HEADS
    q_width = w_kv.shape[-1] // 2
    n_heads = q_width // DIFF_V_DIM
    d_ff2 = w_ffn_up.shape[-1]

    cast = lambda w: w.astype(BF16)
    w_in_a_b, w_out_a_b, w_kv_b = cast(w_in_a), cast(w_out_a), cast(w_kv)
    w_in_b_b, w_out_b_b, w_mem_kv_b = cast(w_in_b), cast(w_out_b), cast(w_mem_kv)
    w_ffn_up_b, w_ffn_down_b = cast(w_ffn_up), cast(w_ffn_down)

    def ffn(x2, l, ffn_prev, seq_len, t_valid):
        h = rmsnorm_cast(x2, g_ffn[l])
        g, st = ffn_up(h, w_ffn_up_b[l], w_ffn_dw[l], b_ffn_dw[l], ffn_prev, seq_len, t_valid)
        return matmul(g, w_ffn_down_b[l], res=x2), st

    def trunk(x, t_valid, pos, mem_k, mem_v, conv_prev, ffn_prev, attend):
        nb, t = x.shape[:2]
        m = nb * t
        x2 = x.reshape(m, d_model)
        tables = rope_tables(pos)
        conv_new, ffn_new = [], []
        k = v = None
        for l in range(depth):
            h = rmsnorm_cast(x2, g_mix[l])
            mk = mem_k[l].reshape(nb, -1, mem_width)
            mv = mem_v[l].reshape(nb, -1, mem_width)
            if l < n_a:
                i = l
                u = matmul(h, w_in_a_b[i]).reshape(nb, t, -1)
                c, st = conv_module(u, conv_prev[i], w_dw_a[i], b_dw_a[i], ln_g_a[i], ln_b_a[i],
                                    conv_dim, t_valid)
                conv_new.append(st)
                mo = mem_attend(u, 2 * conv_dim // mem_width, mk, mv, g_mem_q[l])
                cm = jnp.concatenate([c, mo], axis=-1).reshape(m, -1)
                x2 = matmul(cm, w_out_a_b[i], res=x2)
            else:
                j = l - n_a
                lambda_init = 0.8 - 0.6 * math.exp(-0.3 * l)
                u = matmul(h, w_in_b_b[j]).reshape(nb, t, -1)
                (q,) = norm_rope(u, q_width, g_q_diff[j], tables, (BF16,))
                lam = (jnp.exp(jnp.sum(lambda_q1[j].astype(F32) * lambda_k1[j].astype(F32)))
                       - jnp.exp(jnp.sum(lambda_q2[j].astype(F32) * lambda_k2[j].astype(F32)))
                       + lambda_init)
                o = attend(q, k, v, lam, g_subln[j], 1.0 - lambda_init)
                mo = mem_attend(u, q_width // mem_width, mk, mv, g_mem_q[l])
                om = jnp.concatenate([o, mo], axis=-1).reshape(m, -1)
                x2 = matmul(om, w_out_b_b[j], res=x2)
            x2, st = ffn(x2, l, ffn_prev[l], t, t_valid)
            ffn_new.append(st)
            if l == n_a - 1:
                kv = matmul(rmsnorm_cast(x2, g_kv), w_kv_b).reshape(nb, t, -1)
                k32, k16 = norm_rope(kv, q_width, g_k_diff, tables, (F32, BF16))
                v32 = kv[..., q_width:]
                k, v = (k32, k16), (v32, v32.astype(BF16))
        return x2.reshape(nb, t, d_model), jnp.stack(conv_new), jnp.stack(ffn_new), k[0], v[0]

    bp, sp = x_prompt.shape[:2]
    mem_tokens = mem_prompt.shape[1]
    mem2 = mem_prompt.reshape(bp * mem_tokens, d_model)
    mem_k_l, mem_v_l = [], []
    for l in range(depth):
        kvm = matmul(rmsnorm_cast(mem2, g_mem[l]), w_mem_kv_b[l])
        mem_k_l.append(group_rmsnorm(kvm, mem_width, mem_head_dim, g_mem_k[l]))
        mem_v_l.append(kvm[:, mem_width:])
    mem_shape = (depth, bp, mem_tokens, MEM_HEADS, mem_head_dim)
    mem_k_prompt = jnp.stack(mem_k_l).reshape(mem_shape)
    mem_v_prompt = jnp.stack(mem_v_l).reshape(mem_shape)
    conv0 = jnp.zeros((n_a, bp, CONV_WIDTH - 1, conv_dim), F32)
    ffn0 = jnp.zeros((depth, bp, FFN_CONV_WIDTH - 1, d_ff2), F32)

    def attend_prompt(q, k, v, lam, g, post_scale):
        return diff_attn_prompt(q, k[1], v[1], lam, g, post_scale)

    y_prompt, conv_prompt, ffn_prompt, k_prompt, v_prompt = trunk(
        x_prompt, sp, jnp.arange(sp), mem_k_prompt, mem_v_prompt, conv0, ffn0, attend_prompt)

    bs, ts = x_sample.shape[:2]
    past_len = page_table.shape[1] * PAGE_SIZE
    pad_t = ((0, 0), (0, SAMPLE_T_PAD - ts), (0, 0))
    xs = jnp.pad(x_sample, pad_t)

    def attend_sample(q, k, v, lam, g, post_scale):
        shape5 = (bs, ts, n_heads, 2, DIFF_QK_DIM)
        o = diff_attn_sample(q[:, :ts].reshape(shape5), k[0][:, :ts].reshape(shape5),
                             v[0][:, :ts].reshape(bs, ts, n_heads, DIFF_V_DIM),
                             cache_k, cache_v, page_table, lam, g, post_scale)
        return jnp.pad(o, pad_t)

    y_s, conv_sample, ffn_sample, k_s, v_s = trunk(
        xs, ts, past_len + jnp.arange(SAMPLE_T_PAD), cache_mem_k, cache_mem_v, state_conv,
        state_ffn, attend_sample)

    k5 = lambda k, b, t: k.reshape(b, t, n_heads, 2, DIFF_QK_DIM)
    v4 = lambda v, b, t: v.reshape(b, t, n_heads, DIFF_V_DIM)
    return (y_prompt, y_s[:, :ts], conv_prompt, ffn_prompt,
            k5(k_prompt, bp, sp), v4(v_prompt, bp, sp), mem_k_prompt, mem_v_prompt,
            conv_sample, ffn_sample,
            k5(k_s[:, :ts], bs, ts), v4(v_s[:, :ts], bs, ts))
```

```python
import functools
import math

import jax
import jax.numpy as jnp
from jax import lax
from jax.experimental import pallas as pl
from jax.experimental.pallas import tpu as pltpu

EPS = 1e-6
ROPE_THETA = 500000.0
MEM_HEADS = 4
CONV_WIDTH = 31
FFN_CONV_WIDTH = 3
DIFF_QK_DIM = 128
DIFF_V_DIM = 256
ROT_DIM = DIFF_QK_DIM // 4
PAGE_SIZE = 128

LANES = 128
SUBLANES_F32 = 8
SUBLANES_BF16 = 16
VMEM_LIMIT_BYTES = 56 * 1024 * 1024
MATMUL_VMEM_BUDGET = 44 * 1024 * 1024

CONV_HALO = 32
FFN_HALO = 8
SAMPLE_T_PAD = 16
SAMPLE_PAGES_PER_STEP = 4
SAMPLE_COLS = 128

BF16 = jnp.bfloat16
F32 = jnp.float32


def _params(*sem):
    return pltpu.CompilerParams(dimension_semantics=sem, vmem_limit_bytes=VMEM_LIMIT_BYTES)


def _rmsnorm_cast_body(x_ref, g_ref, o_ref):
    x = x_ref[...]
    y = x * lax.rsqrt(jnp.mean(x * x, axis=-1, keepdims=True) + EPS)
    o_ref[...] = (y * g_ref[...]).astype(o_ref.dtype)


def rmsnorm_cast(x, g):
    m, d = x.shape
    tm = min(m, 256)
    assert m % tm == 0
    return pl.pallas_call(
        _rmsnorm_cast_body,
        grid=(m // tm,),
        in_specs=[pl.BlockSpec((tm, d), lambda i: (i, 0)),
                  pl.BlockSpec((1, d), lambda i: (0, 0))],
        out_specs=pl.BlockSpec((tm, d), lambda i: (i, 0)),
        out_shape=jax.ShapeDtypeStruct((m, d), BF16),
        compiler_params=_params("parallel"),
        name="rmsnorm_cast",
    )(x, g.reshape(1, d))


def _matmul_body(a_ref, w_ref, *rest, has_res):
    acc = jnp.dot(a_ref[...], w_ref[...], preferred_element_type=F32)
    if has_res:
        res_ref, o_ref = rest
        acc = res_ref[...] + acc
    else:
        (o_ref,) = rest
    o_ref[...] = acc.astype(o_ref.dtype)


def _row_tiles(m):
    return [tm for tm in (1024, 512, 256, 128, m) if m % tm == 0 and tm % SUBLANES_BF16 == 0]


def _matmul_tiles(m, k, n, has_res):
    for tm in _row_tiles(m):
        for tn in (1024, 512, 256, 128):
            if n % tn:
                continue
            est = 2 * tm * k * 2 + 2 * k * tn * 2 + 2 * tm * tn * 4 * (2 if has_res else 1)
            if est <= MATMUL_VMEM_BUDGET:
                return tm, tn
    raise ValueError(f"no matmul tiling for {(m, k, n)}")


def matmul(a, w, layer, res=None):
    m, k = a.shape
    n = w.shape[2]
    has_res = res is not None
    tm, tn = _matmul_tiles(m, k, n, has_res)
    in_specs = [pl.BlockSpec((tm, k), lambda i, j: (i, 0)),
                pl.BlockSpec((None, k, tn), lambda i, j: (layer, 0, j))]
    args = [a, w]
    if has_res:
        in_specs.append(pl.BlockSpec((tm, tn), lambda i, j: (i, j)))
        args.append(res)
    return pl.pallas_call(
        functools.partial(_matmul_body, has_res=has_res),
        grid=(m // tm, n // tn),
        in_specs=in_specs,
        out_specs=pl.BlockSpec((tm, tn), lambda i, j: (i, j)),
        out_shape=jax.ShapeDtypeStruct((m, n), F32),
        compiler_params=_params("parallel", "parallel"),
        name="matmul_res" if has_res else "matmul",
    )(*args)


def _ffn_up_body(a_ref, wg_ref, wu_ref, dwg_ref, dwu_ref, bg_ref, bu_ref, pg_ref, pu_ref,
                 o_ref, sg_ref, su_ref, bufg, bufu, wbg, wbu, *, tm, tiles_per_seq, t_valid_last):
    i = pl.program_id(1)

    @pl.when(i == 0)
    def _():
        wbg[...] = wg_ref[...].astype(BF16)
        wbu[...] = wu_ref[...].astype(BF16)

    @pl.when(i % tiles_per_seq == 0)
    def _():
        bufg[0:FFN_HALO, :] = jnp.zeros((FFN_HALO, bufg.shape[1]), F32)
        bufu[0:FFN_HALO, :] = jnp.zeros((FFN_HALO, bufu.shape[1]), F32)
        bufg[FFN_HALO - 2:FFN_HALO, :] = pg_ref[0]
        bufu[FFN_HALO - 2:FFN_HALO, :] = pu_ref[0]

    a = a_ref[...]

    def half(wb, dw_ref, b_ref, buf, s_ref):
        u = jnp.dot(a, wb[...], preferred_element_type=F32)
        buf[FFN_HALO:FFN_HALO + tm, :] = u
        y = (dw_ref[0:1, :] * buf[FFN_HALO - 2:FFN_HALO - 2 + tm, :]
             + dw_ref[1:2, :] * buf[FFN_HALO - 1:FFN_HALO - 1 + tm, :]
             + dw_ref[2:3, :] * u) + b_ref[...]
        s_ref[0] = buf[FFN_HALO + t_valid_last - 2:FFN_HALO + t_valid_last, :]
        buf[0:FFN_HALO, :] = buf[tm:tm + FFN_HALO, :]
        return y

    yg = half(wbg, dwg_ref, bg_ref, bufg, sg_ref)
    yu = half(wbu, dwu_ref, bu_ref, bufu, su_ref)
    o_ref[...] = (yg * jax.nn.sigmoid(yg) * yu).astype(o_ref.dtype)


def ffn_up(a, w_up, w_dw, b_dw, layer, prev, seq_len, t_valid):
    m, k = a.shape
    f2 = w_up.shape[2]
    f = f2 // 2
    tn = 256
    assert f % tn == 0
    nj = f // tn
    tm = min(seq_len, 1024)
    assert seq_len % tm == 0
    tiles_per_seq = seq_len // tm
    nb = m // seq_len
    t_valid_last = t_valid - (tiles_per_seq - 1) * tm
    assert 2 <= t_valid_last <= tm
    b3 = b_dw.reshape(b_dw.shape[0], 1, f2)
    body = functools.partial(_ffn_up_body, tm=tm, tiles_per_seq=tiles_per_seq,
                             t_valid_last=t_valid_last)
    g, sg, su = pl.pallas_call(
        body,
        grid=(nj, m // tm),
        in_specs=[
            pl.BlockSpec((tm, k), lambda j, i: (i, 0)),
            pl.BlockSpec((None, k, tn), lambda j, i: (layer, 0, j)),
            pl.BlockSpec((None, k, tn), lambda j, i: (layer, 0, j + nj)),
            pl.BlockSpec((None, FFN_CONV_WIDTH, tn), lambda j, i: (layer, 0, j)),
            pl.BlockSpec((None, FFN_CONV_WIDTH, tn), lambda j, i: (layer, 0, j + nj)),
            pl.BlockSpec((None, 1, tn), lambda j, i: (layer, 0, j)),
            pl.BlockSpec((None, 1, tn), lambda j, i: (layer, 0, j + nj)),
            pl.BlockSpec((1, 2, tn), lambda j, i: (i // tiles_per_seq, 0, j)),
            pl.BlockSpec((1, 2, tn), lambda j, i: (i // tiles_per_seq, 0, j + nj)),
        ],
        out_specs=[
            pl.BlockSpec((tm, tn), lambda j, i: (i, j)),
            pl.BlockSpec((1, 2, tn), lambda j, i: (i // tiles_per_seq, 0, j)),
            pl.BlockSpec((1, 2, tn), lambda j, i: (i // tiles_per_seq, 0, j)),
        ],
        out_shape=[jax.ShapeDtypeStruct((m, f), BF16),
                   jax.ShapeDtypeStruct((nb, 2, f), F32),
                   jax.ShapeDtypeStruct((nb, 2, f), F32)],
        scratch_shapes=[pltpu.VMEM((tm + FFN_HALO, tn), F32),
                        pltpu.VMEM((tm + FFN_HALO, tn), F32),
                        pltpu.VMEM((k, tn), BF16),
                        pltpu.VMEM((k, tn), BF16)],
        compiler_params=_params("parallel", "arbitrary"),
        name="ffn_up",
    )(a, w_up, w_up, w_dw, w_dw, b3, b3, prev, prev)
    return g, jnp.concatenate([sg, su], axis=-1)


def _conv_module_body(a_ref, gate_ref, prev_ref, w_ref, b_ref, lg_ref, lb_ref,
                      o_ref, st_ref, buf, *, tt, tiles_per_seq, t_valid_last):
    i = pl.program_id(1)

    @pl.when(i % tiles_per_seq == 0)
    def _():
        buf[0:CONV_HALO, :] = prev_ref[0]

    glu = a_ref[0] * jax.nn.sigmoid(gate_ref[0])
    buf[CONV_HALO:CONV_HALO + tt, :] = glu
    base = CONV_HALO - (CONV_WIDTH - 1)
    acc = w_ref[0:1, :] * buf[base:base + tt, :]
    for j in range(1, CONV_WIDTH):
        acc = acc + w_ref[j:j + 1, :] * buf[base + j:base + j + tt, :]
    c = acc + b_ref[...]
    mu = jnp.mean(c, axis=-1, keepdims=True)
    d = c - mu
    var = jnp.mean(d * d, axis=-1, keepdims=True)
    y = d * lax.rsqrt(var + EPS) * lg_ref[...] + lb_ref[...]
    o_ref[0] = (y * jax.nn.sigmoid(y)).astype(o_ref.dtype)
    st_ref[0] = buf[base + t_valid_last:base + t_valid_last + CONV_WIDTH - 1, :]
    buf[0:CONV_HALO, :] = buf[tt:tt + CONV_HALO, :]


def conv_module(u, prev, w_dw, b_dw, ln_g, ln_b, c_dim, t_valid):
    nb, t, _ = u.shape
    tt = min(t, 256)
    assert t % tt == 0
    tiles_per_seq = t // tt
    t_valid_last = t_valid - (tiles_per_seq - 1) * tt
    assert 0 < t_valid_last <= tt
    prev_pad = jnp.pad(prev, ((0, 0), (CONV_HALO - (CONV_WIDTH - 1), 0), (0, 0)))
    body = functools.partial(_conv_module_body, tt=tt, tiles_per_seq=tiles_per_seq,
                             t_valid_last=t_valid_last)
    row = lambda v: v.reshape(1, c_dim)
    return pl.pallas_call(
        body,
        grid=(nb, tiles_per_seq),
        in_specs=[
            pl.BlockSpec((1, tt, c_dim), lambda b, i: (b, i, 0)),
            pl.BlockSpec((1, tt, c_dim), lambda b, i: (b, i, 1)),
            pl.BlockSpec((1, CONV_HALO, c_dim), lambda b, i: (b, 0, 0)),
            pl.BlockSpec((CONV_WIDTH, c_dim), lambda b, i: (0, 0)),
            pl.BlockSpec((1, c_dim), lambda b, i: (0, 0)),
            pl.BlockSpec((1, c_dim), lambda b, i: (0, 0)),
            pl.BlockSpec((1, c_dim), lambda b, i: (0, 0)),
        ],
        out_specs=[
            pl.BlockSpec((1, tt, c_dim), lambda b, i: (b, i, 0)),
            pl.BlockSpec((1, CONV_WIDTH - 1, c_dim), lambda b, i: (b, 0, 0)),
        ],
        out_shape=[jax.ShapeDtypeStruct((nb, t, c_dim), BF16),
                   jax.ShapeDtypeStruct((nb, CONV_WIDTH - 1, c_dim), F32)],
        scratch_shapes=[pltpu.VMEM((tt + CONV_HALO, c_dim), F32)],
        compiler_params=_params("parallel", "arbitrary"),
        name="conv_module",
    )(u, u, prev_pad, w_dw, row(b_dw), row(ln_g), row(ln_b))


def _mem_attend_body(q_ref, k_ref, v_ref, g_ref, o_ref, *, head_dim):
    scale = head_dim ** -0.5
    for h in range(MEM_HEADS):
        sl = slice(h * head_dim, (h + 1) * head_dim)
        q = q_ref[0, :, sl]
        q = q * lax.rsqrt(jnp.mean(q * q, axis=-1, keepdims=True) + EPS) * g_ref[...]
        k = k_ref[0, :, sl].astype(BF16)
        s = lax.dot_general(q.astype(BF16), k, (((1,), (1,)), ((), ())),
                            preferred_element_type=F32) * scale
        s = s - jnp.max(s, axis=-1, keepdims=True)
        e = jnp.exp(s)
        p = e / jnp.sum(e, axis=-1, keepdims=True)
        o = jnp.dot(p.astype(BF16), v_ref[0, :, sl].astype(BF16), preferred_element_type=F32)
        o_ref[0, :, sl] = o.astype(o_ref.dtype)


def mem_attend(u, col_block, mem_k, mem_v, g_q):
    nb, t, _ = u.shape
    mt, w = mem_k.shape[1:]
    head_dim = w // MEM_HEADS
    tq = min(t, 512)
    assert t % tq == 0
    return pl.pallas_call(
        functools.partial(_mem_attend_body, head_dim=head_dim),
        grid=(nb, t // tq),
        in_specs=[
            pl.BlockSpec((1, tq, w), lambda b, i: (b, i, col_block)),
            pl.BlockSpec((1, mt, w), lambda b, i: (b, 0, 0)),
            pl.BlockSpec((1, mt, w), lambda b, i: (b, 0, 0)),
            pl.BlockSpec((1, head_dim), lambda b, i: (0, 0)),
        ],
        out_specs=pl.BlockSpec((1, tq, w), lambda b, i: (b, i, 0)),
        out_shape=jax.ShapeDtypeStruct((nb, t, w), BF16),
        compiler_params=_params("parallel", "parallel"),
        name="mem_attend",
    )(u, mem_k, mem_v, g_q.reshape(1, head_dim))


def _group_norm_body(x_ref, g_ref, o_ref, *, group):
    for h in range(x_ref.shape[1] // group):
        sl = slice(h * group, (h + 1) * group)
        x = x_ref[:, sl]
        y = x * lax.rsqrt(jnp.mean(x * x, axis=-1, keepdims=True) + EPS)
        o_ref[:, sl] = (y * g_ref[...]).astype(o_ref.dtype)


def group_rmsnorm(x, width, group, g):
    m = x.shape[0]
    tm = min(m, 512)
    assert m % tm == 0
    return pl.pallas_call(
        functools.partial(_group_norm_body, group=group),
        grid=(m // tm,),
        in_specs=[pl.BlockSpec((tm, width), lambda i: (i, 0)),
                  pl.BlockSpec((1, group), lambda i: (0, 0))],
        out_specs=pl.BlockSpec((tm, width), lambda i: (i, 0)),
        out_shape=jax.ShapeDtypeStruct((m, width), F32),
        compiler_params=_params("parallel"),
        name="group_rmsnorm",
    )(x, g.reshape(1, group))


def _norm_rope_body(x_ref, g_ref, cos_ref, sa_ref, sb_ref, *o_refs, flat_out):
    half = ROT_DIM // 2
    cos, sa, sb = cos_ref[...], sa_ref[...], sb_ref[...]
    for hc in range(x_ref.shape[2] // DIFF_QK_DIM):
        sl = slice(hc * DIFF_QK_DIM, (hc + 1) * DIFF_QK_DIM)
        x = x_ref[0, :, sl]
        y = x * lax.rsqrt(jnp.mean(x * x, axis=-1, keepdims=True) + EPS) * g_ref[...]
        r = (y * cos + pltpu.roll(y, DIFF_QK_DIM - half, axis=1) * sa
             + pltpu.roll(y, half, axis=1) * sb)
        h, c = divmod(hc, 2)
        heads_ref = o_refs[-1]
        heads_ref[0, h, :, c * DIFF_QK_DIM:(c + 1) * DIFF_QK_DIM] = r.astype(heads_ref.dtype)
        if flat_out:
            o_refs[0][0, :, sl] = r


def rope_tables(pos):
    half = ROT_DIM // 2
    inv = jnp.power(ROPE_THETA, -jnp.arange(0, ROT_DIM, 2, dtype=F32) / ROT_DIM)
    ang = pos.astype(F32)[:, None] * inv[None, :]
    cos, sin = jnp.cos(ang), jnp.sin(ang)
    t = pos.shape[0]
    rest = DIFF_QK_DIM - ROT_DIM
    cos_t = jnp.concatenate([cos, cos, jnp.ones((t, rest), F32)], axis=1)
    sa_t = jnp.concatenate([-sin, jnp.zeros((t, half + rest), F32)], axis=1)
    sb_t = jnp.concatenate([jnp.zeros((t, half), F32), sin, jnp.zeros((t, rest), F32)], axis=1)
    return cos_t, sa_t, sb_t


def norm_rope(x, width, g, tables, flat_out):
    nb, t, _ = x.shape
    nh = width // DIFF_V_DIM
    tt = min(t, 256)
    assert t % tt == 0
    tab_spec = pl.BlockSpec((tt, DIFF_QK_DIM), lambda b, i: (i, 0))
    out_specs = [pl.BlockSpec((1, nh, tt, DIFF_V_DIM), lambda b, i: (b, 0, i, 0))]
    out_shape = [jax.ShapeDtypeStruct((nb, nh, t, DIFF_V_DIM), BF16)]
    if flat_out:
        out_specs.insert(0, pl.BlockSpec((1, tt, width), lambda b, i: (b, i, 0)))
        out_shape.insert(0, jax.ShapeDtypeStruct((nb, t, width), F32))
    return pl.pallas_call(
        functools.partial(_norm_rope_body, flat_out=flat_out),
        grid=(nb, t // tt),
        in_specs=[pl.BlockSpec((1, tt, width), lambda b, i: (b, i, 0)),
                  pl.BlockSpec((1, DIFF_QK_DIM), lambda b, i: (0, 0)),
                  tab_spec, tab_spec, tab_spec],
        out_specs=out_specs,
        out_shape=out_shape,
        compiler_params=_params("parallel", "parallel"),
        name="norm_rope",
    )(x, g.reshape(1, DIFF_QK_DIM), *tables)


def _split_heads_body(x_ref, o32_ref, o16_ref):
    for h in range(o32_ref.shape[1]):
        x = x_ref[0, :, h * DIFF_V_DIM:(h + 1) * DIFF_V_DIM]
        o32_ref[0, h] = x
        o16_ref[0, h] = x.astype(o16_ref.dtype)


def split_heads(x, col_block, width):
    nb, t, _ = x.shape
    nh = width // DIFF_V_DIM
    tt = min(t, 256)
    assert t % tt == 0
    spec = pl.BlockSpec((1, nh, tt, DIFF_V_DIM), lambda b, i: (b, 0, i, 0))
    return pl.pallas_call(
        _split_heads_body,
        grid=(nb, t // tt),
        in_specs=[pl.BlockSpec((1, tt, width), lambda b, i: (b, i, col_block))],
        out_specs=[spec, spec],
        out_shape=[jax.ShapeDtypeStruct((nb, nh, t, DIFF_V_DIM), F32),
                   jax.ShapeDtypeStruct((nb, nh, t, DIFF_V_DIM), BF16)],
        compiler_params=_params("parallel", "parallel"),
        name="split_heads",
    )(x)


def _subln(o, g, post_scale):
    y = o * lax.rsqrt(jnp.mean(o * o, axis=-1, keepdims=True) + EPS)
    return y * g * post_scale


def _diff_attn_prompt_body(lam_ref, q_ref, k_ref, v_ref, g_ref, o_ref, m_sc, l_sc, acc_sc,
                           *, tq, post_scale):
    qi = pl.program_id(2)
    scale = DIFF_QK_DIM ** -0.5
    m_sc[...] = jnp.full(m_sc.shape, -jnp.inf, F32)
    l_sc[...] = jnp.zeros(l_sc.shape, F32)
    acc_sc[...] = jnp.zeros(acc_sc.shape, F32)

    def step(ki, masked):
        start = pl.multiple_of(ki * tq, tq)
        v = v_ref[0, 0, pl.ds(start, tq), :]
        for c in range(2):
            sl = slice(c * DIFF_QK_DIM, (c + 1) * DIFF_QK_DIM)
            s = lax.dot_general(q_ref[0, 0, :, sl], k_ref[0, 0, pl.ds(start, tq), sl],
                                (((1,), (1,)), ((), ())), preferred_element_type=F32) * scale
            if masked:
                row = lax.broadcasted_iota(jnp.int32, s.shape, 0)
                col = lax.broadcasted_iota(jnp.int32, s.shape, 1)
                s = jnp.where(col <= row, s, -jnp.inf)
            m_prev = m_sc[c]
            m_new = jnp.maximum(m_prev, jnp.max(s, axis=-1, keepdims=True))
            alpha = jnp.exp(m_prev - m_new)
            p = jnp.exp(s - m_new)
            l_sc[c] = alpha * l_sc[c] + jnp.sum(p, axis=-1, keepdims=True)
            acc_sc[c] = alpha * acc_sc[c] + jnp.dot(p.astype(BF16), v, preferred_element_type=F32)
            m_sc[c] = m_new

    def full_step(ki, carry):
        step(ki, False)
        return carry

    lax.fori_loop(0, qi, full_step, 0)
    step(qi, True)
    o = acc_sc[0] / l_sc[0] - lam_ref[0, 0] * (acc_sc[1] / l_sc[1])
    o_ref[0] = _subln(o, g_ref[...], post_scale).astype(o_ref.dtype)


def diff_attn_prompt(q, k, v, lam, g_subln, post_scale):
    nb, nh, t, _ = q.shape
    tq = min(t, 512)
    assert t % tq == 0
    body = functools.partial(_diff_attn_prompt_body, tq=tq, post_scale=post_scale)
    kv_spec = pl.BlockSpec((1, 1, t, DIFF_V_DIM), lambda b, h, i: (b, h, 0, 0))
    return pl.pallas_call(
        body,
        grid=(nb, nh, t // tq),
        in_specs=[
            pl.BlockSpec(memory_space=pltpu.SMEM),
            pl.BlockSpec((1, 1, tq, DIFF_V_DIM), lambda b, h, i: (b, h, i, 0)),
            kv_spec, kv_spec,
            pl.BlockSpec((1, DIFF_V_DIM), lambda b, h, i: (0, 0)),
        ],
        out_specs=pl.BlockSpec((1, tq, DIFF_V_DIM), lambda b, h, i: (b, i, h)),
        out_shape=jax.ShapeDtypeStruct((nb, t, nh * DIFF_V_DIM), BF16),
        scratch_shapes=[pltpu.VMEM((2, tq, 1), F32), pltpu.VMEM((2, tq, 1), F32),
                        pltpu.VMEM((2, tq, DIFF_V_DIM), F32)],
        compiler_params=_params("parallel", "parallel", "parallel"),
        name="diff_attn_prompt",
    )(lam.reshape(1, 1), q, k, v, g_subln.reshape(1, DIFF_V_DIM))


def _diff_attn_sample_body(pt_ref, lam_ref, qt_ref, *rest, n_steps, pages_per_step, n_heads,
                           post_scale):
    pp = pages_per_step
    k_refs, v_refs = rest[:pp], rest[pp:2 * pp]
    kn_ref, vn_ref, bias_ref, g_ref, o_ref, st_sc, m_sc, l_sc, acc_sc = rest[2 * pp:]
    p_id = pl.program_id(1)
    scale = DIFF_QK_DIM ** -0.5
    n_sub = 2 * n_heads
    pair = 2 * SUBLANES_F32

    @pl.when(p_id == 0)
    def _():
        m_sc[...] = jnp.full(m_sc.shape, -jnp.inf, F32)
        l_sc[...] = jnp.zeros(l_sc.shape, F32)
        acc_sc[...] = jnp.zeros(acc_sc.shape, F32)

    def update(k_pages, v_pages, bias):
        qt = qt_ref[0]
        for r, kp in enumerate(k_pages):
            st_sc[r] = jnp.dot(kp[...].astype(BF16), qt, preferred_element_type=F32)
        col = lax.broadcasted_iota(jnp.int32, (1, SAMPLE_COLS), 1)
        sub_of_col = ((col >> 4) * 2 + ((col >> 2) & 1)) * 2 + ((col >> 3) & 1)
        s_t = []
        for r in range(len(k_pages)):
            s = jnp.zeros((PAGE_SIZE, SAMPLE_COLS), F32)
            for hc in range(n_sub):
                rows = st_sc[r, pl.ds(hc, PAGE_SIZE, stride=n_sub), :]
                s = jnp.where(sub_of_col == hc, rows, s)
            x = (s * scale).T
            s_t.append(x if bias is None else x + bias)
        m_prev = m_sc[...]
        m_new = m_prev
        for x in s_t:
            m_new = jnp.maximum(m_new, jnp.max(x, axis=-1, keepdims=True))
        alpha = jnp.exp(m_prev - m_new)
        l_new = alpha * l_sc[...]
        p16 = []
        for x in s_t:
            p = jnp.exp(x - m_new)
            l_new = l_new + jnp.sum(p, axis=-1, keepdims=True)
            p16.append(p.astype(BF16))
        l_sc[...] = l_new
        m_sc[...] = m_new
        for h in range(n_heads):
            rows = slice((h // 2) * pair, (h // 2 + 1) * pair)
            pv = None
            for p, vp in zip(p16, v_pages):
                d = jnp.dot(p[rows, :], vp[h].astype(BF16), preferred_element_type=F32)
                pv = d if pv is None else pv + d
            acc_sc[h] = alpha[rows] * acc_sc[h] + pv

    @pl.when(p_id < n_steps)
    def _():
        update([r.at[0] for r in k_refs], [r.at[0] for r in v_refs], None)

    @pl.when(p_id == n_steps)
    def _():
        update([kn_ref.at[0]], [vn_ref.at[0]], bias_ref[...])
        lam = lam_ref[0, 0]
        for h in range(n_heads):
            r0 = (h // 2) * pair
            l0 = l_sc[r0:r0 + SUBLANES_F32, :]
            l1 = l_sc[r0 + SUBLANES_F32:r0 + pair, :]
            o = (acc_sc[h, 0:SUBLANES_F32, :] / l0
                 - lam * (acc_sc[h, SUBLANES_F32:pair, :] / l1))
            o_ref[0, h] = _subln(o, g_ref[...], post_scale).astype(o_ref.dtype)


def diff_attn_sample(q, k_new, v_new, cache_k, cache_v, page_table, lam, g_subln, post_scale):
    nb, t, nh = q.shape[:3]
    n_sub = 2 * nh
    n_pages = page_table.shape[1]
    pp = SAMPLE_PAGES_PER_STEP if n_pages % SAMPLE_PAGES_PER_STEP == 0 else 1
    n_steps = n_pages // pp
    assert nh % 2 == 0 and 2 * t == SUBLANES_F32 and n_sub * t <= SAMPLE_COLS
    n_pairs = nh // 2
    qt = q.reshape(nb, t, n_pairs, 2, 2, DIFF_QK_DIM).transpose(0, 5, 2, 4, 3, 1)
    qt = qt.reshape(nb, DIFF_QK_DIM, n_sub * t)
    qt = jnp.pad(qt, ((0, 0), (0, 0), (0, SAMPLE_COLS - n_sub * t))).astype(BF16)
    page_rows = PAGE_SIZE * n_sub
    kn = jnp.pad(k_new.reshape(nb, t * n_sub, DIFF_QK_DIM), ((0, 0), (0, page_rows - t * n_sub), (0, 0)))
    vn = jnp.pad(v_new, ((0, 0), (0, 0), (0, PAGE_SIZE - t), (0, 0)))
    col_q = jnp.arange(SAMPLE_COLS) % t
    tok = jnp.arange(PAGE_SIZE)
    bias = jnp.where((tok[None, :] <= col_q[:, None]) & (tok[None, :] < t), 0.0, -jnp.inf).astype(F32)
    ck = cache_k.reshape(cache_k.shape[0], page_rows, DIFF_QK_DIM)
    cv = cache_v.transpose(0, 2, 1, 3)

    def page_index(p, pt, b, r):
        return pt[b, jnp.minimum(p, n_steps - 1) * pp + r]

    def k_spec(r):
        return pl.BlockSpec((1, page_rows, DIFF_QK_DIM),
                            lambda b, p, pt: (page_index(p, pt, b, r), 0, 0))

    def v_spec(r):
        return pl.BlockSpec((1, nh, PAGE_SIZE, DIFF_V_DIM),
                            lambda b, p, pt: (page_index(p, pt, b, r), 0, 0, 0))

    body = functools.partial(_diff_attn_sample_body, n_steps=n_steps, pages_per_step=pp,
                             n_heads=nh, post_scale=post_scale)
    grid_spec = pltpu.PrefetchScalarGridSpec(
        num_scalar_prefetch=1,
        grid=(nb, n_steps + 1),
        in_specs=(
            [pl.BlockSpec(memory_space=pltpu.SMEM),
             pl.BlockSpec((1, DIFF_QK_DIM, SAMPLE_COLS), lambda b, p, pt: (b, 0, 0))]
            + [k_spec(r) for r in range(pp)] + [v_spec(r) for r in range(pp)]
            + [pl.BlockSpec((1, page_rows, DIFF_QK_DIM), lambda b, p, pt: (b, 0, 0)),
               pl.BlockSpec((1, nh, PAGE_SIZE, DIFF_V_DIM), lambda b, p, pt: (b, 0, 0, 0)),
               pl.BlockSpec((SAMPLE_COLS, PAGE_SIZE), lambda b, p, pt: (0, 0)),
               pl.BlockSpec((1, DIFF_V_DIM), lambda b, p, pt: (0, 0))]),
        out_specs=pl.BlockSpec((1, nh, SUBLANES_F32, DIFF_V_DIM), lambda b, p, pt: (b, 0, 0, 0)),
        scratch_shapes=[pltpu.VMEM((pp, page_rows, SAMPLE_COLS), F32),
                        pltpu.VMEM((SAMPLE_COLS, 1), F32), pltpu.VMEM((SAMPLE_COLS, 1), F32),
                        pltpu.VMEM((nh, 2 * SUBLANES_F32, DIFF_V_DIM), F32)],
    )
    out = pl.pallas_call(
        body,
        grid_spec=grid_spec,
        out_shape=jax.ShapeDtypeStruct((nb, nh, SUBLANES_F32, DIFF_V_DIM), BF16),
        compiler_params=_params("parallel", "arbitrary"),
        name="diff_attn_sample",
    )(page_table, lam.reshape(1, 1), qt, *([ck] * pp), *([cv] * pp), kn, vn, bias,
      g_subln.reshape(1, DIFF_V_DIM))
    out = out.reshape(nb, n_pairs, 2, 2, t, DIFF_V_DIM)
    out = jnp.stack([out[:, :, 0, 0], out[:, :, 1, 1]], axis=2)
    return out.transpose(0, 3, 1, 2, 4).reshape(nb, t, nh * DIFF_V_DIM)


def kernel(x_prompt, x_sample, mem_prompt, state_conv, state_ffn, cache_k, cache_v, cache_mem_k, cache_mem_v, page_table, g_mix, g_ffn, w_in_a, w_dw_a, b_dw_a, ln_g_a, ln_b_a, w_out_a, g_kv, w_kv, g_k_diff, w_in_b, g_q_diff, lambda_q1, lambda_k1, lambda_q2, lambda_k2, g_subln, w_out_b, g_mem, w_mem_kv, g_mem_q, g_mem_k, w_ffn_up, w_ffn_dw, b_ffn_dw, w_ffn_down):
    depth = g_mix.shape[0]
    n_a = w_in_a.shape[0]
    d_model = x_prompt.shape[-1]
    conv_dim = w_dw_a.shape[-1]
    mem_width = w_mem_kv.shape[-1] // 2
    mem_head_dim = mem_width // MEM_HEADS
    q_width = w_kv.shape[-1] // 2
    n_heads = q_width // DIFF_V_DIM

    cast = lambda w: w.astype(BF16)
    w_in_a_b, w_out_a_b, w_in_b_b, w_out_b_b = cast(w_in_a), cast(w_out_a), cast(w_in_b), cast(w_out_b)
    w_kv_b, w_mem_kv_b, w_ffn_down_b = cast(w_kv.reshape(1, *w_kv.shape)), cast(w_mem_kv), cast(w_ffn_down)

    def ffn(x2, l, ffn_prev, seq_len, t_valid):
        h = rmsnorm_cast(x2, g_ffn[l])
        g, st = ffn_up(h, w_ffn_up, w_ffn_dw, b_ffn_dw, l, ffn_prev, seq_len, t_valid)
        return matmul(g, w_ffn_down_b, l, res=x2), st

    def trunk(x, t_valid, pos, mem_k, mem_v, conv_prev, ffn_prev, attend):
        nb, t = x.shape[:2]
        m = nb * t
        x2 = x.reshape(m, d_model)
        tables = rope_tables(pos)
        conv_new, ffn_new = [], []
        k = v = None
        for l in range(depth):
            h = rmsnorm_cast(x2, g_mix[l])
            mk = mem_k[l].reshape(nb, -1, mem_width)
            mv = mem_v[l].reshape(nb, -1, mem_width)
            if l < n_a:
                i = l
                u = matmul(h, w_in_a_b, i).reshape(nb, t, -1)
                c, st = conv_module(u, conv_prev[i], w_dw_a[i], b_dw_a[i], ln_g_a[i], ln_b_a[i],
                                    conv_dim, t_valid)
                conv_new.append(st)
                mo = mem_attend(u, 2 * conv_dim // mem_width, mk, mv, g_mem_q[l])
                cm = jnp.concatenate([c, mo], axis=-1).reshape(m, -1)
                x2 = matmul(cm, w_out_a_b, i, res=x2)
            else:
                j = l - n_a
                lambda_init = 0.8 - 0.6 * math.exp(-0.3 * l)
                u = matmul(h, w_in_b_b, j).reshape(nb, t, -1)
                lam = (jnp.exp(jnp.sum(lambda_q1[j].astype(F32) * lambda_k1[j].astype(F32)))
                       - jnp.exp(jnp.sum(lambda_q2[j].astype(F32) * lambda_k2[j].astype(F32)))
                       + lambda_init)
                o = attend(u, g_q_diff[j], tables, k, v, lam, g_subln[j], 1.0 - lambda_init)
                mo = mem_attend(u, q_width // mem_width, mk, mv, g_mem_q[l])
                om = jnp.concatenate([o, mo], axis=-1).reshape(m, -1)
                x2 = matmul(om, w_out_b_b, j, res=x2)
            x2, st = ffn(x2, l, ffn_prev[l], t, t_valid)
            ffn_new.append(st)
            if l == n_a - 1:
                kv = matmul(rmsnorm_cast(x2, g_kv), w_kv_b, 0).reshape(nb, t, -1)
                k = norm_rope(kv, q_width, g_k_diff, tables, flat_out=True)
                v = split_heads(kv, 1, q_width)
        return x2.reshape(nb, t, d_model), jnp.stack(conv_new), jnp.stack(ffn_new), k[0], v[0]

    bp, sp = x_prompt.shape[:2]
    mem_tokens = mem_prompt.shape[1]
    mem2 = mem_prompt.reshape(bp * mem_tokens, d_model)
    mem_k_l, mem_v_l = [], []
    for l in range(depth):
        kvm = matmul(rmsnorm_cast(mem2, g_mem[l]), w_mem_kv_b, l)
        mem_k_l.append(group_rmsnorm(kvm, mem_width, mem_head_dim, g_mem_k[l]))
        mem_v_l.append(kvm[:, mem_width:])
    mem_shape = (depth, bp, mem_tokens, MEM_HEADS, mem_head_dim)
    mem_k_prompt = jnp.stack(mem_k_l).reshape(mem_shape)
    mem_v_prompt = jnp.stack(mem_v_l).reshape(mem_shape)
    conv0 = jnp.zeros((n_a, bp, CONV_WIDTH - 1, conv_dim), F32)
    ffn0 = jnp.zeros((depth, bp, FFN_CONV_WIDTH - 1, w_ffn_up.shape[-1]), F32)

    def attend_prompt(u, g_q, tables, k, v, lam, g, post_scale):
        (q,) = norm_rope(u, q_width, g_q, tables, flat_out=False)
        return diff_attn_prompt(q, k[1], v[1], lam, g, post_scale)

    y_prompt, conv_prompt, ffn_prompt, k_prompt, v_prompt = trunk(
        x_prompt, sp, jnp.arange(sp), mem_k_prompt, mem_v_prompt, conv0, ffn0, attend_prompt)

    bs, ts = x_sample.shape[:2]
    past_len = page_table.shape[1] * PAGE_SIZE
    pad_t = ((0, 0), (0, SAMPLE_T_PAD - ts), (0, 0))
    xs = jnp.pad(x_sample, pad_t)
    shape5 = (bs, ts, n_heads, 2, DIFF_QK_DIM)

    def attend_sample(u, g_q, tables, k, v, lam, g, post_scale):
        (q,) = norm_rope(u, q_width, g_q, tables, flat_out=False)
        q5 = q[:, :, :ts].astype(F32).reshape(bs, n_heads, ts, 2, DIFF_QK_DIM).transpose(0, 2, 1, 3, 4)
        o = diff_attn_sample(q5, k[0][:, :ts].reshape(shape5), v[0][:, :, :ts],
                             cache_k, cache_v, page_table, lam, g, post_scale)
        return jnp.pad(o, pad_t)

    y_s, conv_sample, ffn_sample, k_s, v_s = trunk(
        xs, ts, past_len + jnp.arange(SAMPLE_T_PAD), cache_mem_k, cache_mem_v, state_conv,
        state_ffn, attend_sample)

    return (y_prompt, y_s[:, :ts], conv_prompt, ffn_prompt,
            k_prompt.reshape(bp, sp, n_heads, 2, DIFF_QK_DIM), v_prompt.transpose(0, 2, 1, 3),
            mem_k_prompt, mem_v_prompt, conv_sample, ffn_sample,
            k_s[:, :ts].reshape(shape5), v_s[:, :, :ts].transpose(0, 2, 1, 3))
```

```python
import functools
import math

import jax
import jax.numpy as jnp
from jax import lax
from jax.experimental import pallas as pl
from jax.experimental.pallas import tpu as pltpu

EPS = 1e-6
ROPE_THETA = 500000.0
MEM_HEADS = 4
CONV_WIDTH = 31
FFN_CONV_WIDTH = 3
DIFF_QK_DIM = 128
DIFF_V_DIM = 256
ROT_DIM = DIFF_QK_DIM // 4
PAGE_SIZE = 128

LANES = 128
SUBLANES_F32 = 8
SUBLANES_BF16 = 16
VMEM_LIMIT_BYTES = 56 * 1024 * 1024
MATMUL_VMEM_BUDGET = 44 * 1024 * 1024

CONV_HALO = 32
CONV_ROW_BLOCK = 64
FFN_HALO = 8
SAMPLE_T_PAD = 16
SAMPLE_PAGES_PER_STEP = 4
SAMPLE_COLS = 128

BF16 = jnp.bfloat16
F32 = jnp.float32


def _params(*sem):
    return pltpu.CompilerParams(dimension_semantics=sem, vmem_limit_bytes=VMEM_LIMIT_BYTES)


def _rmsnorm_cast_body(x_ref, g_ref, o_ref):
    x = x_ref[...]
    y = x * lax.rsqrt(jnp.mean(x * x, axis=-1, keepdims=True) + EPS)
    o_ref[...] = (y * g_ref[...]).astype(o_ref.dtype)


def rmsnorm_cast(x, g):
    m, d = x.shape
    tm = min(m, 256)
    assert m % tm == 0
    return pl.pallas_call(
        _rmsnorm_cast_body,
        grid=(m // tm,),
        in_specs=[pl.BlockSpec((tm, d), lambda i: (i, 0)),
                  pl.BlockSpec((1, d), lambda i: (0, 0))],
        out_specs=pl.BlockSpec((tm, d), lambda i: (i, 0)),
        out_shape=jax.ShapeDtypeStruct((m, d), BF16),
        compiler_params=_params("parallel"),
        name="rmsnorm_cast",
    )(x, g.reshape(1, d))


def _matmul_body(a_ref, w_ref, *rest, has_res):
    acc = jnp.dot(a_ref[...], w_ref[...], preferred_element_type=F32)
    if has_res:
        res_ref, o_ref = rest
        acc = res_ref[...] + acc
    else:
        (o_ref,) = rest
    o_ref[...] = acc.astype(o_ref.dtype)


def _row_tiles(m):
    return [tm for tm in (1024, 512, 256, 128, m) if m % tm == 0 and tm % SUBLANES_BF16 == 0]


def _matmul_tiles(m, k, n, has_res):
    for tm in _row_tiles(m):
        for tn in (1024, 512, 256, 128):
            if n % tn:
                continue
            est = 2 * tm * k * 2 + 2 * k * tn * 2 + 2 * tm * tn * 4 * (2 if has_res else 1)
            if est <= MATMUL_VMEM_BUDGET:
                return tm, tn
    raise ValueError(f"no matmul tiling for {(m, k, n)}")


def matmul(a, w, layer, res=None):
    m, k = a.shape
    n = w.shape[2]
    has_res = res is not None
    tm, tn = _matmul_tiles(m, k, n, has_res)
    in_specs = [pl.BlockSpec((tm, k), lambda i, j: (i, 0)),
                pl.BlockSpec((None, k, tn), lambda i, j: (layer, 0, j))]
    args = [a, w]
    if has_res:
        in_specs.append(pl.BlockSpec((tm, tn), lambda i, j: (i, j)))
        args.append(res)
    return pl.pallas_call(
        functools.partial(_matmul_body, has_res=has_res),
        grid=(m // tm, n // tn),
        in_specs=in_specs,
        out_specs=pl.BlockSpec((tm, tn), lambda i, j: (i, j)),
        out_shape=jax.ShapeDtypeStruct((m, n), F32),
        compiler_params=_params("parallel", "parallel"),
        name="matmul_res" if has_res else "matmul",
    )(*args)


def _ffn_up_body(a_ref, wg_ref, wu_ref, dwg_ref, dwu_ref, bg_ref, bu_ref, pg_ref, pu_ref,
                 o_ref, sg_ref, su_ref, bufg, bufu, wbg, wbu, *, chunk, n_chunks, seq_per_tile,
                 tiles_per_seq, t_valid_chunk):
    i = pl.program_id(1)

    @pl.when(i == 0)
    def _():
        wbg[...] = wg_ref[...].astype(BF16)
        wbu[...] = wu_ref[...].astype(BF16)

    halves = ((wbg, dwg_ref, bg_ref, bufg, pg_ref, sg_ref), (wbu, dwu_ref, bu_ref, bufu, pu_ref, su_ref))

    def start_sequence(s):
        for _, _, _, buf, p_ref, _ in halves:
            buf[0:FFN_HALO, :] = jnp.zeros((FFN_HALO, buf.shape[1]), F32)
            buf[FFN_HALO - 2:FFN_HALO, :] = p_ref[s]

    if not seq_per_tile:
        pl.when(i % tiles_per_seq == 0)(lambda: start_sequence(0))

    for c in range(n_chunks):
        if seq_per_tile:
            start_sequence(c)
        rows = slice(c * chunk, (c + 1) * chunk)
        a = a_ref[rows, :]
        ys = []
        for wb, dw_ref, b_ref, buf, _, s_ref in halves:
            u = jnp.dot(a, wb[...], preferred_element_type=F32)
            buf[FFN_HALO:FFN_HALO + chunk, :] = u
            ys.append(dw_ref[0:1, :] * buf[FFN_HALO - 2:FFN_HALO - 2 + chunk, :]
                      + dw_ref[1:2, :] * buf[FFN_HALO - 1:FFN_HALO - 1 + chunk, :]
                      + dw_ref[2:3, :] * u + b_ref[...])
            if seq_per_tile or c == n_chunks - 1:
                s_ref[c if seq_per_tile else 0] = buf[FFN_HALO + t_valid_chunk - 2:FFN_HALO + t_valid_chunk, :]
            if not seq_per_tile:
                buf[0:FFN_HALO, :] = buf[chunk:chunk + FFN_HALO, :]
        yg, yu = ys
        o_ref[rows, :] = (yg * jax.nn.sigmoid(yg) * yu).astype(o_ref.dtype)


FFN_ROW_TILE = 1024
FFN_ROW_CHUNK = 256


def ffn_up(a, w_up, w_dw, b_dw, layer, prev, seq_len, t_valid):
    m, k = a.shape
    f2 = w_up.shape[2]
    f = f2 // 2
    tn = 256
    assert f % tn == 0
    nj = f // tn
    nb = m // seq_len
    if seq_len >= FFN_ROW_TILE:
        tm, chunk, seq_per_tile = FFN_ROW_TILE, FFN_ROW_CHUNK, 0
        assert seq_len % tm == 0 and t_valid == seq_len
        tiles_per_seq = seq_len // tm
        t_valid_chunk = chunk
    else:
        seq_per_tile = min(nb, FFN_ROW_TILE // seq_len)
        assert nb % seq_per_tile == 0 and seq_len % SUBLANES_BF16 == 0
        tm, chunk, tiles_per_seq = seq_per_tile * seq_len, seq_len, 1
        t_valid_chunk = t_valid
    assert 2 <= t_valid_chunk <= chunk
    seq_block = max(seq_per_tile, 1)
    seq_of_tile = (lambda i: i) if seq_per_tile else (lambda i: i // tiles_per_seq)
    b3 = b_dw.reshape(b_dw.shape[0], 1, f2)
    body = functools.partial(_ffn_up_body, chunk=chunk, n_chunks=tm // chunk, seq_per_tile=seq_per_tile,
                             tiles_per_seq=tiles_per_seq, t_valid_chunk=t_valid_chunk)
    g, sg, su = pl.pallas_call(
        body,
        grid=(nj, m // tm),
        in_specs=[
            pl.BlockSpec((tm, k), lambda j, i: (i, 0)),
            pl.BlockSpec((None, k, tn), lambda j, i: (layer, 0, j)),
            pl.BlockSpec((None, k, tn), lambda j, i: (layer, 0, j + nj)),
            pl.BlockSpec((None, FFN_CONV_WIDTH, tn), lambda j, i: (layer, 0, j)),
            pl.BlockSpec((None, FFN_CONV_WIDTH, tn), lambda j, i: (layer, 0, j + nj)),
            pl.BlockSpec((None, 1, tn), lambda j, i: (layer, 0, j)),
            pl.BlockSpec((None, 1, tn), lambda j, i: (layer, 0, j + nj)),
            pl.BlockSpec((seq_block, 2, tn), lambda j, i: (seq_of_tile(i), 0, j)),
            pl.BlockSpec((seq_block, 2, tn), lambda j, i: (seq_of_tile(i), 0, j + nj)),
        ],
        out_specs=[
            pl.BlockSpec((tm, tn), lambda j, i: (i, j)),
            pl.BlockSpec((seq_block, 2, tn), lambda j, i: (seq_of_tile(i), 0, j)),
            pl.BlockSpec((seq_block, 2, tn), lambda j, i: (seq_of_tile(i), 0, j)),
        ],
        out_shape=[jax.ShapeDtypeStruct((m, f), BF16),
                   jax.ShapeDtypeStruct((nb, 2, f), F32),
                   jax.ShapeDtypeStruct((nb, 2, f), F32)],
        scratch_shapes=[pltpu.VMEM((chunk + FFN_HALO, tn), F32),
                        pltpu.VMEM((chunk + FFN_HALO, tn), F32),
                        pltpu.VMEM((k, tn), BF16),
                        pltpu.VMEM((k, tn), BF16)],
        compiler_params=_params("parallel", "arbitrary"),
        name="ffn_up",
    )(a, w_up, w_up, w_dw, w_dw, b3, b3, prev, prev)
    return g, jnp.concatenate([sg, su], axis=-1)


def _conv_module_body(a_ref, gate_ref, prev_ref, w_ref, b_ref, lg_ref, lb_ref,
                      o_ref, st_ref, buf, cbuf, wrep, *, tt, tiles_per_seq, t_valid_last):
    i = pl.program_id(1)

    @pl.when(i % tiles_per_seq == 0)
    def _():
        buf[0:CONV_HALO, :] = prev_ref[0]

    glu = a_ref[0] * jax.nn.sigmoid(gate_ref[0])
    buf[CONV_HALO:CONV_HALO + tt, :] = glu
    base = CONV_HALO - (CONV_WIDTH - 1)
    rb = min(tt, CONV_ROW_BLOCK)
    sub = SUBLANES_F32

    @pl.when(i == 0)
    def _():
        for j in range(CONV_WIDTH):
            wrep[j * sub:(j + 1) * sub, :] = jnp.broadcast_to(w_ref[j:j + 1, :], (sub, wrep.shape[1]))
        wrep[CONV_WIDTH * sub:(CONV_WIDTH + 1) * sub, :] = jnp.broadcast_to(b_ref[...], (sub, wrep.shape[1]))

    def row_block(r, carry):
        r0 = pl.multiple_of(r * rb, rb)
        for l0 in range(0, buf.shape[1], LANES):
            lanes = slice(l0, l0 + LANES)
            xa = buf[pl.ds(r0, rb + CONV_HALO), lanes]
            acc = [wrep[CONV_WIDTH * sub:(CONV_WIDTH + 1) * sub, lanes]] * (rb // sub)
            for b in range(sub):
                taps = list(range(b, CONV_WIDTH, sub))
                sb = xa[base + b:base + b + rb + sub * (len(taps) - 1)]
                for a, j in enumerate(taps):
                    w8 = wrep[j * sub:(j + 1) * sub, lanes]
                    acc = [acc[v] + w8 * sb[sub * (a + v):sub * (a + v + 1)] for v in range(rb // sub)]
            cbuf[pl.ds(r0, rb), lanes] = jnp.concatenate(acc, axis=0)
        return carry

    lax.fori_loop(0, tt // rb, row_block, 0)
    c = cbuf[...]
    mu = jnp.mean(c, axis=-1, keepdims=True)
    d = c - mu
    var = jnp.mean(d * d, axis=-1, keepdims=True)
    y = d * lax.rsqrt(var + EPS) * lg_ref[...] + lb_ref[...]
    o_ref[0] = (y * jax.nn.sigmoid(y)).astype(o_ref.dtype)
    st_ref[0] = buf[base + t_valid_last:base + t_valid_last + CONV_WIDTH - 1, :]
    buf[0:CONV_HALO, :] = buf[tt:tt + CONV_HALO, :]


def conv_module(u, prev, w_dw, b_dw, ln_g, ln_b, c_dim, t_valid):
    nb, t, _ = u.shape
    tt = min(t, 256)
    assert t % tt == 0
    tiles_per_seq = t // tt
    t_valid_last = t_valid - (tiles_per_seq - 1) * tt
    assert 0 < t_valid_last <= tt
    prev_pad = jnp.pad(prev, ((0, 0), (CONV_HALO - (CONV_WIDTH - 1), 0), (0, 0)))
    body = functools.partial(_conv_module_body, tt=tt, tiles_per_seq=tiles_per_seq,
                             t_valid_last=t_valid_last)
    row = lambda v: v.reshape(1, c_dim)
    return pl.pallas_call(
        body,
        grid=(nb, tiles_per_seq),
        in_specs=[
            pl.BlockSpec((1, tt, c_dim), lambda b, i: (b, i, 0)),
            pl.BlockSpec((1, tt, c_dim), lambda b, i: (b, i, 1)),
            pl.BlockSpec((1, CONV_HALO, c_dim), lambda b, i: (b, 0, 0)),
            pl.BlockSpec((CONV_WIDTH, c_dim), lambda b, i: (0, 0)),
            pl.BlockSpec((1, c_dim), lambda b, i: (0, 0)),
            pl.BlockSpec((1, c_dim), lambda b, i: (0, 0)),
            pl.BlockSpec((1, c_dim), lambda b, i: (0, 0)),
        ],
        out_specs=[
            pl.BlockSpec((1, tt, c_dim), lambda b, i: (b, i, 0)),
            pl.BlockSpec((1, CONV_WIDTH - 1, c_dim), lambda b, i: (b, 0, 0)),
        ],
        out_shape=[jax.ShapeDtypeStruct((nb, t, c_dim), BF16),
                   jax.ShapeDtypeStruct((nb, CONV_WIDTH - 1, c_dim), F32)],
        scratch_shapes=[pltpu.VMEM((tt + CONV_HALO, c_dim), F32), pltpu.VMEM((tt, c_dim), F32),
                        pltpu.VMEM(((CONV_WIDTH + 1) * SUBLANES_F32, c_dim), F32)],
        compiler_params=_params("parallel", "arbitrary"),
        name="conv_module",
    )(u, u, prev_pad, w_dw, row(b_dw), row(ln_g), row(ln_b))


def _mem_attend_body(q_ref, k_ref, v_ref, g_ref, o_ref, *, head_dim):
    scale = head_dim ** -0.5
    for h in range(MEM_HEADS):
        sl = slice(h * head_dim, (h + 1) * head_dim)
        q = q_ref[0, :, sl]
        q = q * lax.rsqrt(jnp.mean(q * q, axis=-1, keepdims=True) + EPS) * g_ref[...]
        k = k_ref[0, :, sl].astype(BF16)
        s = lax.dot_general(q.astype(BF16), k, (((1,), (1,)), ((), ())),
                            preferred_element_type=F32) * scale
        s = s - jnp.max(s, axis=-1, keepdims=True)
        e = jnp.exp(s)
        p = e / jnp.sum(e, axis=-1, keepdims=True)
        o = jnp.dot(p.astype(BF16), v_ref[0, :, sl].astype(BF16), preferred_element_type=F32)
        o_ref[0, :, sl] = o.astype(o_ref.dtype)


def mem_attend(u, col_block, mem_k, mem_v, g_q):
    nb, t, _ = u.shape
    mt, w = mem_k.shape[1:]
    head_dim = w // MEM_HEADS
    tq = min(t, 512)
    assert t % tq == 0
    return pl.pallas_call(
        functools.partial(_mem_attend_body, head_dim=head_dim),
        grid=(nb, t // tq),
        in_specs=[
            pl.BlockSpec((1, tq, w), lambda b, i: (b, i, col_block)),
            pl.BlockSpec((1, mt, w), lambda b, i: (b, 0, 0)),
            pl.BlockSpec((1, mt, w), lambda b, i: (b, 0, 0)),
            pl.BlockSpec((1, head_dim), lambda b, i: (0, 0)),
        ],
        out_specs=pl.BlockSpec((1, tq, w), lambda b, i: (b, i, 0)),
        out_shape=jax.ShapeDtypeStruct((nb, t, w), BF16),
        compiler_params=_params("parallel", "parallel"),
        name="mem_attend",
    )(u, mem_k, mem_v, g_q.reshape(1, head_dim))


def _group_norm_body(x_ref, g_ref, o_ref, *, group):
    for h in range(x_ref.shape[1] // group):
        sl = slice(h * group, (h + 1) * group)
        x = x_ref[:, sl]
        y = x * lax.rsqrt(jnp.mean(x * x, axis=-1, keepdims=True) + EPS)
        o_ref[:, sl] = (y * g_ref[...]).astype(o_ref.dtype)


def group_rmsnorm(x, width, group, g):
    m = x.shape[0]
    tm = min(m, 512)
    assert m % tm == 0
    return pl.pallas_call(
        functools.partial(_group_norm_body, group=group),
        grid=(m // tm,),
        in_specs=[pl.BlockSpec((tm, width), lambda i: (i, 0)),
                  pl.BlockSpec((1, group), lambda i: (0, 0))],
        out_specs=pl.BlockSpec((tm, width), lambda i: (i, 0)),
        out_shape=jax.ShapeDtypeStruct((m, width), F32),
        compiler_params=_params("parallel"),
        name="group_rmsnorm",
    )(x, g.reshape(1, group))


def _norm_rope_body(x_ref, g_ref, cos_ref, sa_ref, sb_ref, *o_refs, flat_out, head_scale):
    half = ROT_DIM // 2
    cos, sa, sb = cos_ref[...], sa_ref[...], sb_ref[...]
    n_sub = x_ref.shape[2] // DIFF_QK_DIM
    for hc in range(n_sub):
        sl = slice(hc * DIFF_QK_DIM, (hc + 1) * DIFF_QK_DIM)
        x = x_ref[0, :, sl]
        y = x * lax.rsqrt(jnp.mean(x * x, axis=-1, keepdims=True) + EPS) * g_ref[...]
        r = (y * cos + pltpu.roll(y, DIFF_QK_DIM - half, axis=1) * sa
             + pltpu.roll(y, half, axis=1) * sb)
        h, c = divmod(hc, 2)
        heads_ref = o_refs[-1]
        heads_ref[0, h, :, c * DIFF_QK_DIM:(c + 1) * DIFF_QK_DIM] = (r * head_scale).astype(heads_ref.dtype)
        if flat_out:
            o_refs[0][0, pl.ds(hc, x_ref.shape[1], stride=n_sub), :] = r


def rope_tables(pos):
    half = ROT_DIM // 2
    inv = jnp.power(ROPE_THETA, -jnp.arange(0, ROT_DIM, 2, dtype=F32) / ROT_DIM)
    ang = pos.astype(F32)[:, None] * inv[None, :]
    cos, sin = jnp.cos(ang), jnp.sin(ang)
    t = pos.shape[0]
    rest = DIFF_QK_DIM - ROT_DIM
    cos_t = jnp.concatenate([cos, cos, jnp.ones((t, rest), F32)], axis=1)
    sa_t = jnp.concatenate([-sin, jnp.zeros((t, half + rest), F32)], axis=1)
    sb_t = jnp.concatenate([jnp.zeros((t, half), F32), sin, jnp.zeros((t, rest), F32)], axis=1)
    return cos_t, sa_t, sb_t


def norm_rope(x, width, g, tables, flat_out, head_scale=1.0):
    nb, t, _ = x.shape
    nh = width // DIFF_V_DIM
    n_sub = width // DIFF_QK_DIM
    tt = min(t, 256)
    assert t % tt == 0
    tab_spec = pl.BlockSpec((tt, DIFF_QK_DIM), lambda b, i: (i, 0))
    out_specs = [pl.BlockSpec((1, nh, tt, DIFF_V_DIM), lambda b, i: (b, 0, i, 0))]
    out_shape = [jax.ShapeDtypeStruct((nb, nh, t, DIFF_V_DIM), BF16)]
    if flat_out:
        out_specs.insert(0, pl.BlockSpec((1, tt * n_sub, DIFF_QK_DIM), lambda b, i: (b, i, 0)))
        out_shape.insert(0, jax.ShapeDtypeStruct((nb, t * n_sub, DIFF_QK_DIM), F32))
    return pl.pallas_call(
        functools.partial(_norm_rope_body, flat_out=flat_out, head_scale=head_scale),
        grid=(nb, t // tt),
        in_specs=[pl.BlockSpec((1, tt, width), lambda b, i: (b, i, 0)),
                  pl.BlockSpec((1, DIFF_QK_DIM), lambda b, i: (0, 0)),
                  tab_spec, tab_spec, tab_spec],
        out_specs=out_specs,
        out_shape=out_shape,
        compiler_params=_params("parallel", "parallel"),
        name="norm_rope",
    )(x, g.reshape(1, DIFF_QK_DIM), *tables)


def _split_heads_body(x_ref, o32_ref, o16_ref):
    for h in range(o32_ref.shape[1]):
        x = x_ref[0, :, h * DIFF_V_DIM:(h + 1) * DIFF_V_DIM]
        o32_ref[0, h] = x
        o16_ref[0, h] = x.astype(o16_ref.dtype)


def split_heads(x, col_block, width):
    nb, t, _ = x.shape
    nh = width // DIFF_V_DIM
    tt = min(t, 256)
    assert t % tt == 0
    spec = pl.BlockSpec((1, nh, tt, DIFF_V_DIM), lambda b, i: (b, 0, i, 0))
    return pl.pallas_call(
        _split_heads_body,
        grid=(nb, t // tt),
        in_specs=[pl.BlockSpec((1, tt, width), lambda b, i: (b, i, col_block))],
        out_specs=[spec, spec],
        out_shape=[jax.ShapeDtypeStruct((nb, nh, t, DIFF_V_DIM), F32),
                   jax.ShapeDtypeStruct((nb, nh, t, DIFF_V_DIM), BF16)],
        compiler_params=_params("parallel", "parallel"),
        name="split_heads",
    )(x)


def _subln(o, g, post_scale):
    y = o * lax.rsqrt(jnp.mean(o * o, axis=-1, keepdims=True) + EPS)
    return y * g * post_scale


def _diff_attn_prompt_body(lam_ref, q_ref, k_ref, v_ref, g_ref, o_ref, m_sc, l_sc, acc_sc,
                           *, tq, post_scale):
    qi = pl.program_id(2)
    m_sc[...] = jnp.full(m_sc.shape, -jnp.inf, F32)
    l_sc[...] = jnp.zeros(l_sc.shape, F32)
    acc_sc[...] = jnp.zeros(acc_sc.shape, F32)

    def step(ki, masked):
        start = pl.multiple_of(ki * tq, tq)
        v = v_ref[0, 0, pl.ds(start, tq), :]
        for c in range(2):
            sl = slice(c * DIFF_QK_DIM, (c + 1) * DIFF_QK_DIM)
            s = lax.dot_general(q_ref[0, 0, :, sl], k_ref[0, 0, pl.ds(start, tq), sl],
                                (((1,), (1,)), ((), ())), preferred_element_type=F32)
            if masked:
                row = lax.broadcasted_iota(jnp.int32, s.shape, 0)
                col = lax.broadcasted_iota(jnp.int32, s.shape, 1)
                s = jnp.where(col <= row, s, -jnp.inf)
            m_prev = m_sc[c]
            m_new = jnp.maximum(m_prev, jnp.max(s, axis=-1, keepdims=True))
            alpha = jnp.exp2(m_prev - m_new)
            p = jnp.exp2(s - m_new)
            l_sc[c] = alpha * l_sc[c] + jnp.sum(p, axis=-1, keepdims=True)
            acc_sc[c] = alpha * acc_sc[c] + jnp.dot(p.astype(BF16), v, preferred_element_type=F32)
            m_sc[c] = m_new

    def full_step(ki, carry):
        step(ki, False)
        return carry

    lax.fori_loop(0, qi, full_step, 0)
    step(qi, True)
    o = acc_sc[0] / l_sc[0] - lam_ref[0, 0] * (acc_sc[1] / l_sc[1])
    o_ref[0] = _subln(o, g_ref[...], post_scale).astype(o_ref.dtype)


def diff_attn_prompt(q, k, v, lam, g_subln, post_scale):
    nb, nh, t, _ = q.shape
    tq = min(t, 512)
    assert t % tq == 0
    body = functools.partial(_diff_attn_prompt_body, tq=tq, post_scale=post_scale)
    kv_spec = pl.BlockSpec((1, 1, t, DIFF_V_DIM), lambda b, h, i: (b, h, 0, 0))
    return pl.pallas_call(
        body,
        grid=(nb, nh, t // tq),
        in_specs=[
            pl.BlockSpec(memory_space=pltpu.SMEM),
            pl.BlockSpec((1, 1, tq, DIFF_V_DIM), lambda b, h, i: (b, h, i, 0)),
            kv_spec, kv_spec,
            pl.BlockSpec((1, DIFF_V_DIM), lambda b, h, i: (0, 0)),
        ],
        out_specs=pl.BlockSpec((1, tq, DIFF_V_DIM), lambda b, h, i: (b, i, h)),
        out_shape=jax.ShapeDtypeStruct((nb, t, nh * DIFF_V_DIM), BF16),
        scratch_shapes=[pltpu.VMEM((2, tq, 1), F32), pltpu.VMEM((2, tq, 1), F32),
                        pltpu.VMEM((2, tq, DIFF_V_DIM), F32)],
        compiler_params=_params("parallel", "parallel", "parallel"),
        name="diff_attn_prompt",
    )(lam.reshape(1, 1), q, k, v, g_subln.reshape(1, DIFF_V_DIM))


def _diff_attn_sample_body(pt_ref, lam_ref, qt_ref, *rest, n_steps, pages_per_step, n_heads,
                           post_scale):
    pp = pages_per_step
    k_refs, v_refs = rest[:pp], rest[pp:2 * pp]
    kn_ref, vn_ref, bias_ref, g_ref, o_ref, st_sc, m_sc, l_sc, acc_sc = rest[2 * pp:]
    p_id = pl.program_id(1)
    scale = DIFF_QK_DIM ** -0.5
    n_sub = 2 * n_heads
    pair = 2 * SUBLANES_F32

    @pl.when(p_id == 0)
    def _():
        m_sc[...] = jnp.full(m_sc.shape, -jnp.inf, F32)
        l_sc[...] = jnp.zeros(l_sc.shape, F32)
        acc_sc[...] = jnp.zeros(acc_sc.shape, F32)

    def update(k_pages, v_pages, bias):
        qt = qt_ref[0]
        for r, kp in enumerate(k_pages):
            st_sc[r] = jnp.dot(kp[...].astype(BF16), qt, preferred_element_type=F32)
        col = lax.broadcasted_iota(jnp.int32, (1, SAMPLE_COLS), 1)
        sub_of_col = ((col >> 4) * 2 + ((col >> 2) & 1)) * 2 + ((col >> 3) & 1)
        s_t = []
        for r in range(len(k_pages)):
            s = jnp.zeros((PAGE_SIZE, SAMPLE_COLS), F32)
            for hc in range(n_sub):
                rows = st_sc[r, pl.ds(hc, PAGE_SIZE, stride=n_sub), :]
                s = jnp.where(sub_of_col == hc, rows, s)
            x = (s * scale).T
            s_t.append(x if bias is None else x + bias)
        m_prev = m_sc[...]
        m_new = m_prev
        for x in s_t:
            m_new = jnp.maximum(m_new, jnp.max(x, axis=-1, keepdims=True))
        alpha = jnp.exp(m_prev - m_new)
        l_new = alpha * l_sc[...]
        p16 = []
        for x in s_t:
            p = jnp.exp(x - m_new)
            l_new = l_new + jnp.sum(p, axis=-1, keepdims=True)
            p16.append(p.astype(BF16))
        l_sc[...] = l_new
        m_sc[...] = m_new
        for h in range(n_heads):
            rows = slice((h // 2) * pair, (h // 2 + 1) * pair)
            pv = None
            for p, vp in zip(p16, v_pages):
                d = jnp.dot(p[rows, :], vp[h].astype(BF16), preferred_element_type=F32)
                pv = d if pv is None else pv + d
            acc_sc[h] = alpha[rows] * acc_sc[h] + pv

    @pl.when(p_id < n_steps)
    def _():
        update([r.at[0] for r in k_refs], [r.at[0] for r in v_refs], None)

    @pl.when(p_id == n_steps)
    def _():
        update([kn_ref.at[0]], [vn_ref.at[0]], bias_ref[...])
        lam = lam_ref[0, 0]
        for h in range(n_heads):
            r0 = (h // 2) * pair
            l0 = l_sc[r0:r0 + SUBLANES_F32, :]
            l1 = l_sc[r0 + SUBLANES_F32:r0 + pair, :]
            o = (acc_sc[h, 0:SUBLANES_F32, :] / l0
                 - lam * (acc_sc[h, SUBLANES_F32:pair, :] / l1))
            o_ref[0, h] = _subln(o, g_ref[...], post_scale).astype(o_ref.dtype)


def diff_attn_sample(q, k_new, v_new, cache_k, cache_v, page_table, lam, g_subln, post_scale):
    nb, t, nh = q.shape[:3]
    n_sub = 2 * nh
    n_pages = page_table.shape[1]
    pp = SAMPLE_PAGES_PER_STEP if n_pages % SAMPLE_PAGES_PER_STEP == 0 else 1
    n_steps = n_pages // pp
    assert nh % 2 == 0 and 2 * t == SUBLANES_F32 and n_sub * t <= SAMPLE_COLS
    n_pairs = nh // 2
    qt = q.reshape(nb, t, n_pairs, 2, 2, DIFF_QK_DIM).transpose(0, 5, 2, 4, 3, 1)
    qt = qt.reshape(nb, DIFF_QK_DIM, n_sub * t)
    qt = jnp.pad(qt, ((0, 0), (0, 0), (0, SAMPLE_COLS - n_sub * t))).astype(BF16)
    page_rows = PAGE_SIZE * n_sub
    kn = jnp.pad(k_new.reshape(nb, t * n_sub, DIFF_QK_DIM), ((0, 0), (0, page_rows - t * n_sub), (0, 0)))
    vn = jnp.pad(v_new, ((0, 0), (0, 0), (0, PAGE_SIZE - t), (0, 0)))
    col_q = jnp.arange(SAMPLE_COLS) % t
    tok = jnp.arange(PAGE_SIZE)
    bias = jnp.where((tok[None, :] <= col_q[:, None]) & (tok[None, :] < t), 0.0, -jnp.inf).astype(F32)
    ck = cache_k.reshape(cache_k.shape[0], page_rows, DIFF_QK_DIM)
    cv = cache_v.transpose(0, 2, 1, 3)

    def page_index(p, pt, b, r):
        return pt[b, jnp.minimum(p, n_steps - 1) * pp + r]

    def k_spec(r):
        return pl.BlockSpec((1, page_rows, DIFF_QK_DIM),
                            lambda b, p, pt: (page_index(p, pt, b, r), 0, 0))

    def v_spec(r):
        return pl.BlockSpec((1, nh, PAGE_SIZE, DIFF_V_DIM),
                            lambda b, p, pt: (page_index(p, pt, b, r), 0, 0, 0))

    body = functools.partial(_diff_attn_sample_body, n_steps=n_steps, pages_per_step=pp,
                             n_heads=nh, post_scale=post_scale)
    grid_spec = pltpu.PrefetchScalarGridSpec(
        num_scalar_prefetch=1,
        grid=(nb, n_steps + 1),
        in_specs=(
            [pl.BlockSpec(memory_space=pltpu.SMEM),
             pl.BlockSpec((1, DIFF_QK_DIM, SAMPLE_COLS), lambda b, p, pt: (b, 0, 0))]
            + [k_spec(r) for r in range(pp)] + [v_spec(r) for r in range(pp)]
            + [pl.BlockSpec((1, page_rows, DIFF_QK_DIM), lambda b, p, pt: (b, 0, 0)),
               pl.BlockSpec((1, nh, PAGE_SIZE, DIFF_V_DIM), lambda b, p, pt: (b, 0, 0, 0)),
               pl.BlockSpec((SAMPLE_COLS, PAGE_SIZE), lambda b, p, pt: (0, 0)),
               pl.BlockSpec((1, DIFF_V_DIM), lambda b, p, pt: (0, 0))]),
        out_specs=pl.BlockSpec((1, nh, SUBLANES_F32, DIFF_V_DIM), lambda b, p, pt: (b, 0, 0, 0)),
        scratch_shapes=[pltpu.VMEM((pp, page_rows, SAMPLE_COLS), F32),
                        pltpu.VMEM((SAMPLE_COLS, 1), F32), pltpu.VMEM((SAMPLE_COLS, 1), F32),
                        pltpu.VMEM((nh, 2 * SUBLANES_F32, DIFF_V_DIM), F32)],
    )
    out = pl.pallas_call(
        body,
        grid_spec=grid_spec,
        out_shape=jax.ShapeDtypeStruct((nb, nh, SUBLANES_F32, DIFF_V_DIM), BF16),
        compiler_params=_params("parallel", "arbitrary"),
        name="diff_attn_sample",
    )(page_table, lam.reshape(1, 1), qt, *([ck] * pp), *([cv] * pp), kn, vn, bias,
      g_subln.reshape(1, DIFF_V_DIM))
    out = out.reshape(nb, n_pairs, 2, 2, t, DIFF_V_DIM)
    out = jnp.stack([out[:, :, 0, 0], out[:, :, 1, 1]], axis=2)
    return out.transpose(0, 3, 1, 2, 4).reshape(nb, t, nh * DIFF_V_DIM)


def kernel(x_prompt, x_sample, mem_prompt, state_conv, state_ffn, cache_k, cache_v, cache_mem_k, cache_mem_v, page_table, g_mix, g_ffn, w_in_a, w_dw_a, b_dw_a, ln_g_a, ln_b_a, w_out_a, g_kv, w_kv, g_k_diff, w_in_b, g_q_diff, lambda_q1, lambda_k1, lambda_q2, lambda_k2, g_subln, w_out_b, g_mem, w_mem_kv, g_mem_q, g_mem_k, w_ffn_up, w_ffn_dw, b_ffn_dw, w_ffn_down):
    depth = g_mix.shape[0]
    n_a = w_in_a.shape[0]
    d_model = x_prompt.shape[-1]
    conv_dim = w_dw_a.shape[-1]
    mem_width = w_mem_kv.shape[-1] // 2
    mem_head_dim = mem_width // MEM_HEADS
    q_width = w_kv.shape[-1] // 2
    n_heads = q_width // DIFF_V_DIM

    cast = lambda w: w.astype(BF16)
    w_in_a_b, w_out_a_b, w_in_b_b, w_out_b_b = cast(w_in_a), cast(w_out_a), cast(w_in_b), cast(w_out_b)
    w_kv_b, w_mem_kv_b, w_ffn_down_b = cast(w_kv.reshape(1, *w_kv.shape)), cast(w_mem_kv), cast(w_ffn_down)

    def ffn(x2, l, ffn_prev, seq_len, t_valid):
        h = rmsnorm_cast(x2, g_ffn[l])
        g, st = ffn_up(h, w_ffn_up, w_ffn_dw, b_ffn_dw, l, ffn_prev, seq_len, t_valid)
        return matmul(g, w_ffn_down_b, l, res=x2), st

    def trunk(x, t_valid, pos, mem_k, mem_v, conv_prev, ffn_prev, attend):
        nb, t = x.shape[:2]
        m = nb * t
        x2 = x.reshape(m, d_model)
        tables = rope_tables(pos)
        conv_new, ffn_new = [], []
        k = v = None
        for l in range(depth):
            h = rmsnorm_cast(x2, g_mix[l])
            mk = mem_k[l].reshape(nb, -1, mem_width)
            mv = mem_v[l].reshape(nb, -1, mem_width)
            if l < n_a:
                i = l
                u = matmul(h, w_in_a_b, i).reshape(nb, t, -1)
                c, st = conv_module(u, conv_prev[i], w_dw_a[i], b_dw_a[i], ln_g_a[i], ln_b_a[i],
                                    conv_dim, t_valid)
                conv_new.append(st)
                mo = mem_attend(u, 2 * conv_dim // mem_width, mk, mv, g_mem_q[l])
                cm = jnp.concatenate([c, mo], axis=-1).reshape(m, -1)
                x2 = matmul(cm, w_out_a_b, i, res=x2)
            else:
                j = l - n_a
                lambda_init = 0.8 - 0.6 * math.exp(-0.3 * l)
                u = matmul(h, w_in_b_b, j).reshape(nb, t, -1)
                lam = (jnp.exp(jnp.sum(lambda_q1[j].astype(F32) * lambda_k1[j].astype(F32)))
                       - jnp.exp(jnp.sum(lambda_q2[j].astype(F32) * lambda_k2[j].astype(F32)))
                       + lambda_init)
                o = attend(u, g_q_diff[j], tables, k, v, lam, g_subln[j], 1.0 - lambda_init)
                mo = mem_attend(u, q_width // mem_width, mk, mv, g_mem_q[l])
                om = jnp.concatenate([o, mo], axis=-1).reshape(m, -1)
                x2 = matmul(om, w_out_b_b, j, res=x2)
            x2, st = ffn(x2, l, ffn_prev[l], t, t_valid)
            ffn_new.append(st)
            if l == n_a - 1:
                kv = matmul(rmsnorm_cast(x2, g_kv), w_kv_b, 0).reshape(nb, t, -1)
                k = norm_rope(kv, q_width, g_k_diff, tables, flat_out=True)
                v = split_heads(kv, 1, q_width)
        return x2.reshape(nb, t, d_model), jnp.stack(conv_new), jnp.stack(ffn_new), k[0], v[0]

    bp, sp = x_prompt.shape[:2]
    mem_tokens = mem_prompt.shape[1]
    mem2 = mem_prompt.reshape(bp * mem_tokens, d_model)
    mem_k_l, mem_v_l = [], []
    for l in range(depth):
        kvm = matmul(rmsnorm_cast(mem2, g_mem[l]), w_mem_kv_b, l)
        mem_k_l.append(group_rmsnorm(kvm, mem_width, mem_head_dim, g_mem_k[l]))
        mem_v_l.append(kvm[:, mem_width:])
    mem_shape = (depth, bp, mem_tokens, MEM_HEADS, mem_head_dim)
    mem_k_prompt = jnp.stack(mem_k_l).reshape(mem_shape)
    mem_v_prompt = jnp.stack(mem_v_l).reshape(mem_shape)
    conv0 = jnp.zeros((n_a, bp, CONV_WIDTH - 1, conv_dim), F32)
    ffn0 = jnp.zeros((depth, bp, FFN_CONV_WIDTH - 1, w_ffn_up.shape[-1]), F32)

    def attend_prompt(u, g_q, tables, k, v, lam, g, post_scale):
        (q,) = norm_rope(u, q_width, g_q, tables, flat_out=False,
                         head_scale=DIFF_QK_DIM ** -0.5 * math.log2(math.e))
        return diff_attn_prompt(q, k[1], v[1], lam, g, post_scale)

    y_prompt, conv_prompt, ffn_prompt, k_prompt, v_prompt = trunk(
        x_prompt, sp, jnp.arange(sp), mem_k_prompt, mem_v_prompt, conv0, ffn0, attend_prompt)

    bs, ts = x_sample.shape[:2]
    past_len = page_table.shape[1] * PAGE_SIZE
    pad_t = ((0, 0), (0, SAMPLE_T_PAD - ts), (0, 0))
    xs = jnp.pad(x_sample, pad_t)
    shape5 = (bs, ts, n_heads, 2, DIFF_QK_DIM)

    def attend_sample(u, g_q, tables, k, v, lam, g, post_scale):
        (q,) = norm_rope(u, q_width, g_q, tables, flat_out=False)
        q5 = q[:, :, :ts].astype(F32).reshape(bs, n_heads, ts, 2, DIFF_QK_DIM).transpose(0, 2, 1, 3, 4)
        o = diff_attn_sample(q5, k[0][:, :ts * 2 * n_heads].reshape(shape5), v[0][:, :, :ts],
                             cache_k, cache_v, page_table, lam, g, post_scale)
        return jnp.pad(o, pad_t)

    y_s, conv_sample, ffn_sample, k_s, v_s = trunk(
        xs, ts, past_len + jnp.arange(SAMPLE_T_PAD), cache_mem_k, cache_mem_v, state_conv,
        state_ffn, attend_sample)

    return (y_prompt, y_s[:, :ts], conv_prompt, ffn_prompt,
            k_prompt.reshape(bp, sp, n_heads, 2, DIFF_QK_DIM), v_prompt.transpose(0, 2, 1, 3),
            mem_k_prompt, mem_v_prompt, conv_sample, ffn_sample,
            k_s[:, :ts * 2 * n_heads].reshape(shape5), v_s[:, :, :ts].transpose(0, 2, 1, 3))
```

```python
import functools
import math

import jax
import jax.numpy as jnp
from jax import lax
from jax.experimental import pallas as pl
from jax.experimental.pallas import tpu as pltpu

EPS = 1e-6
ROPE_THETA = 500000.0
MEM_HEADS = 4
CONV_WIDTH = 31
FFN_CONV_WIDTH = 3
DIFF_QK_DIM = 128
DIFF_V_DIM = 256
ROT_DIM = DIFF_QK_DIM // 4
PAGE_SIZE = 128

LANES = 128
SUBLANES_F32 = 8
SUBLANES_BF16 = 16
VMEM_LIMIT_BYTES = 56 * 1024 * 1024
MATMUL_VMEM_BUDGET = 50 * 1024 * 1024

CONV_HALO = 32
CONV_ROW_BLOCK = 64
FFN_HALO = 8
SAMPLE_T_PAD = 16
SAMPLE_PAGES_PER_STEP = 4
SAMPLE_COLS = 128

BF16 = jnp.bfloat16
F32 = jnp.float32


def _params(*sem):
    return pltpu.CompilerParams(dimension_semantics=sem, vmem_limit_bytes=VMEM_LIMIT_BYTES)


def _rmsnorm_cast_body(x_ref, g_ref, o_ref):
    x = x_ref[...]
    y = x * lax.rsqrt(jnp.mean(x * x, axis=-1, keepdims=True) + EPS)
    o_ref[...] = (y * g_ref[...]).astype(o_ref.dtype)


def rmsnorm_cast(x, g):
    m, d = x.shape
    tm = min(m, 256)
    assert m % tm == 0
    return pl.pallas_call(
        _rmsnorm_cast_body,
        grid=(m // tm,),
        in_specs=[pl.BlockSpec((tm, d), lambda i: (i, 0)),
                  pl.BlockSpec((1, d), lambda i: (0, 0))],
        out_specs=pl.BlockSpec((tm, d), lambda i: (i, 0)),
        out_shape=jax.ShapeDtypeStruct((m, d), BF16),
        compiler_params=_params("parallel"),
        name="rmsnorm_cast",
    )(x, g.reshape(1, d))


def _matmul_body(a_ref, w_ref, *rest, has_res):
    acc = jnp.dot(a_ref[...], w_ref[...], preferred_element_type=F32)
    if has_res:
        res_ref, o_ref = rest
        acc = res_ref[...] + acc
    else:
        (o_ref,) = rest
    o_ref[...] = acc.astype(o_ref.dtype)


def _row_tiles(m):
    return [tm for tm in (1024, 512, 256, 128, m) if m % tm == 0 and tm % SUBLANES_BF16 == 0]


def _matmul_tiles(m, k, n, has_res):
    for tm in _row_tiles(m):
        for tn in (1024, 512, 256, 128):
            if n % tn:
                continue
            est = 2 * tm * k * 2 + 2 * k * tn * 2 + 2 * tm * tn * 4 * (2 if has_res else 1)
            if est <= MATMUL_VMEM_BUDGET:
                return tm, tn
    raise ValueError(f"no matmul tiling for {(m, k, n)}")


def matmul(a, w, layer, res=None):
    m, k = a.shape
    n = w.shape[2]
    has_res = res is not None
    tm, tn = _matmul_tiles(m, k, n, has_res)
    in_specs = [pl.BlockSpec((tm, k), lambda i, j: (i, 0)),
                pl.BlockSpec((None, k, tn), lambda i, j: (layer, 0, j))]
    args = [a, w]
    if has_res:
        in_specs.append(pl.BlockSpec((tm, tn), lambda i, j: (i, j)))
        args.append(res)
    return pl.pallas_call(
        functools.partial(_matmul_body, has_res=has_res),
        grid=(m // tm, n // tn),
        in_specs=in_specs,
        out_specs=pl.BlockSpec((tm, tn), lambda i, j: (i, j)),
        out_shape=jax.ShapeDtypeStruct((m, n), F32),
        compiler_params=_params("parallel", "parallel"),
        name="matmul_res" if has_res else "matmul",
    )(*args)


def _ffn_up_body(a_ref, wg_ref, wu_ref, dwg_ref, dwu_ref, bg_ref, bu_ref, pg_ref, pu_ref,
                 o_ref, sg_ref, su_ref, bufg, bufu, wbg, wbu, *, chunk, n_chunks, seq_per_tile,
                 tiles_per_seq, t_valid_chunk):
    i = pl.program_id(1)

    @pl.when(i == 0)
    def _():
        wbg[...] = wg_ref[...].astype(BF16)
        wbu[...] = wu_ref[...].astype(BF16)

    halves = ((wbg, dwg_ref, bg_ref, bufg, pg_ref, sg_ref), (wbu, dwu_ref, bu_ref, bufu, pu_ref, su_ref))

    def start_sequence(s):
        for _, _, _, buf, p_ref, _ in halves:
            buf[0:FFN_HALO, :] = jnp.zeros((FFN_HALO, buf.shape[1]), F32)
            buf[FFN_HALO - 2:FFN_HALO, :] = p_ref[s]

    if not seq_per_tile:
        pl.when(i % tiles_per_seq == 0)(lambda: start_sequence(0))

    if seq_per_tile:
        u_tile = [jnp.dot(a_ref[...], wb[...], preferred_element_type=F32) for wb, *_ in halves]

    for c in range(n_chunks):
        if seq_per_tile:
            start_sequence(c)
        rows = slice(c * chunk, (c + 1) * chunk)
        ys = []
        for idx, (wb, dw_ref, b_ref, buf, _, s_ref) in enumerate(halves):
            if seq_per_tile:
                u = u_tile[idx][rows]
            else:
                u = jnp.dot(a_ref[rows, :], wb[...], preferred_element_type=F32)
            buf[FFN_HALO:FFN_HALO + chunk, :] = u
            ys.append(dw_ref[0:1, :] * buf[FFN_HALO - 2:FFN_HALO - 2 + chunk, :]
                      + dw_ref[1:2, :] * buf[FFN_HALO - 1:FFN_HALO - 1 + chunk, :]
                      + dw_ref[2:3, :] * u + b_ref[...])
            if seq_per_tile or c == n_chunks - 1:
                s_ref[c if seq_per_tile else 0] = buf[FFN_HALO + t_valid_chunk - 2:FFN_HALO + t_valid_chunk, :]
            if not seq_per_tile:
                buf[0:FFN_HALO, :] = buf[chunk:chunk + FFN_HALO, :]
        yg, yu = ys
        o_ref[rows, :] = (yg * jax.nn.sigmoid(yg) * yu).astype(o_ref.dtype)


FFN_ROW_TILE = 1024
FFN_ROW_CHUNK = 256


def ffn_up(a, w_up, w_dw, b_dw, layer, prev, seq_len, t_valid):
    m, k = a.shape
    f2 = w_up.shape[2]
    f = f2 // 2
    tn = 256
    assert f % tn == 0
    nj = f // tn
    nb = m // seq_len
    if seq_len >= FFN_ROW_TILE:
        tm, chunk, seq_per_tile = FFN_ROW_TILE, FFN_ROW_CHUNK, 0
        assert seq_len % tm == 0 and t_valid == seq_len
        tiles_per_seq = seq_len // tm
        t_valid_chunk = chunk
    else:
        seq_per_tile = min(nb, FFN_ROW_TILE // seq_len)
        assert nb % seq_per_tile == 0 and seq_len % SUBLANES_BF16 == 0
        tm, chunk, tiles_per_seq = seq_per_tile * seq_len, seq_len, 1
        t_valid_chunk = t_valid
    assert 2 <= t_valid_chunk <= chunk
    seq_block = max(seq_per_tile, 1)
    seq_of_tile = (lambda i: i) if seq_per_tile else (lambda i: i // tiles_per_seq)
    b3 = b_dw.reshape(b_dw.shape[0], 1, f2)
    body = functools.partial(_ffn_up_body, chunk=chunk, n_chunks=tm // chunk, seq_per_tile=seq_per_tile,
                             tiles_per_seq=tiles_per_seq, t_valid_chunk=t_valid_chunk)
    g, sg, su = pl.pallas_call(
        body,
        grid=(nj, m // tm),
        in_specs=[
            pl.BlockSpec((tm, k), lambda j, i: (i, 0)),
            pl.BlockSpec((None, k, tn), lambda j, i: (layer, 0, j)),
            pl.BlockSpec((None, k, tn), lambda j, i: (layer, 0, j + nj)),
            pl.BlockSpec((None, FFN_CONV_WIDTH, tn), lambda j, i: (layer, 0, j)),
            pl.BlockSpec((None, FFN_CONV_WIDTH, tn), lambda j, i: (layer, 0, j + nj)),
            pl.BlockSpec((None, 1, tn), lambda j, i: (layer, 0, j)),
            pl.BlockSpec((None, 1, tn), lambda j, i: (layer, 0, j + nj)),
            pl.BlockSpec((seq_block, 2, tn), lambda j, i: (seq_of_tile(i), 0, j)),
            pl.BlockSpec((seq_block, 2, tn), lambda j, i: (seq_of_tile(i), 0, j + nj)),
        ],
        out_specs=[
            pl.BlockSpec((tm, tn), lambda j, i: (i, j)),
            pl.BlockSpec((seq_block, 2, tn), lambda j, i: (seq_of_tile(i), 0, j)),
            pl.BlockSpec((seq_block, 2, tn), lambda j, i: (seq_of_tile(i), 0, j)),
        ],
        out_shape=[jax.ShapeDtypeStruct((m, f), BF16),
                   jax.ShapeDtypeStruct((nb, 2, f), F32),
                   jax.ShapeDtypeStruct((nb, 2, f), F32)],
        scratch_shapes=[pltpu.VMEM((chunk + FFN_HALO, tn), F32),
                        pltpu.VMEM((chunk + FFN_HALO, tn), F32),
                        pltpu.VMEM((k, tn), BF16),
                        pltpu.VMEM((k, tn), BF16)],
        compiler_params=_params("parallel", "arbitrary"),
        name="ffn_up",
    )(a, w_up, w_up, w_dw, w_dw, b3, b3, prev, prev)
    return g, jnp.concatenate([sg, su], axis=-1)


def _conv_module_body(a_ref, gate_ref, prev_ref, w_ref, b_ref, lg_ref, lb_ref,
                      o_ref, st_ref, buf, cbuf, wrep, *, tt, tiles_per_seq, t_valid_last):
    i = pl.program_id(1)

    @pl.when(i % tiles_per_seq == 0)
    def _():
        buf[0:CONV_HALO, :] = prev_ref[0]

    glu = a_ref[0] * jax.nn.sigmoid(gate_ref[0])
    buf[CONV_HALO:CONV_HALO + tt, :] = glu
    base = CONV_HALO - (CONV_WIDTH - 1)
    rb = min(tt, CONV_ROW_BLOCK)
    sub = SUBLANES_F32

    @pl.when(i == 0)
    def _():
        for j in range(CONV_WIDTH):
            wrep[j * sub:(j + 1) * sub, :] = jnp.broadcast_to(w_ref[j:j + 1, :], (sub, wrep.shape[1]))
        wrep[CONV_WIDTH * sub:(CONV_WIDTH + 1) * sub, :] = jnp.broadcast_to(b_ref[...], (sub, wrep.shape[1]))

    def row_block(r, carry):
        r0 = pl.multiple_of(r * rb, rb)
        for l0 in range(0, buf.shape[1], LANES):
            lanes = slice(l0, l0 + LANES)
            xa = buf[pl.ds(r0, rb + CONV_HALO), lanes]
            acc = [wrep[CONV_WIDTH * sub:(CONV_WIDTH + 1) * sub, lanes]] * (rb // sub)
            for b in range(sub):
                taps = list(range(b, CONV_WIDTH, sub))
                sb = xa[base + b:base + b + rb + sub * (len(taps) - 1)]
                for a, j in enumerate(taps):
                    w8 = wrep[j * sub:(j + 1) * sub, lanes]
                    acc = [acc[v] + w8 * sb[sub * (a + v):sub * (a + v + 1)] for v in range(rb // sub)]
            cbuf[pl.ds(r0, rb), lanes] = jnp.concatenate(acc, axis=0)
        return carry

    lax.fori_loop(0, tt // rb, row_block, 0)
    c = cbuf[...]
    mu = jnp.mean(c, axis=-1, keepdims=True)
    d = c - mu
    var = jnp.mean(d * d, axis=-1, keepdims=True)
    y = d * lax.rsqrt(var + EPS) * lg_ref[...] + lb_ref[...]
    o_ref[0] = (y * jax.nn.sigmoid(y)).astype(o_ref.dtype)
    st_ref[0] = buf[base + t_valid_last:base + t_valid_last + CONV_WIDTH - 1, :]
    buf[0:CONV_HALO, :] = buf[tt:tt + CONV_HALO, :]


def conv_module(u, prev, w_dw, b_dw, ln_g, ln_b, c_dim, t_valid):
    nb, t, _ = u.shape
    tt = min(t, 256)
    assert t % tt == 0
    tiles_per_seq = t // tt
    t_valid_last = t_valid - (tiles_per_seq - 1) * tt
    assert 0 < t_valid_last <= tt
    prev_pad = jnp.pad(prev, ((0, 0), (CONV_HALO - (CONV_WIDTH - 1), 0), (0, 0)))
    body = functools.partial(_conv_module_body, tt=tt, tiles_per_seq=tiles_per_seq,
                             t_valid_last=t_valid_last)
    row = lambda v: v.reshape(1, c_dim)
    return pl.pallas_call(
        body,
        grid=(nb, tiles_per_seq),
        in_specs=[
            pl.BlockSpec((1, tt, c_dim), lambda b, i: (b, i, 0)),
            pl.BlockSpec((1, tt, c_dim), lambda b, i: (b, i, 1)),
            pl.BlockSpec((1, CONV_HALO, c_dim), lambda b, i: (b, 0, 0)),
            pl.BlockSpec((CONV_WIDTH, c_dim), lambda b, i: (0, 0)),
            pl.BlockSpec((1, c_dim), lambda b, i: (0, 0)),
            pl.BlockSpec((1, c_dim), lambda b, i: (0, 0)),
            pl.BlockSpec((1, c_dim), lambda b, i: (0, 0)),
        ],
        out_specs=[
            pl.BlockSpec((1, tt, c_dim), lambda b, i: (b, i, 0)),
            pl.BlockSpec((1, CONV_WIDTH - 1, c_dim), lambda b, i: (b, 0, 0)),
        ],
        out_shape=[jax.ShapeDtypeStruct((nb, t, c_dim), BF16),
                   jax.ShapeDtypeStruct((nb, CONV_WIDTH - 1, c_dim), F32)],
        scratch_shapes=[pltpu.VMEM((tt + CONV_HALO, c_dim), F32), pltpu.VMEM((tt, c_dim), F32),
                        pltpu.VMEM(((CONV_WIDTH + 1) * SUBLANES_F32, c_dim), F32)],
        compiler_params=_params("parallel", "arbitrary"),
        name="conv_module",
    )(u, u, prev_pad, w_dw, row(b_dw), row(ln_g), row(ln_b))


def _mem_attend_body(q_ref, k_ref, v_ref, g_ref, o_ref, *, head_dim):
    scale = head_dim ** -0.5
    for h in range(MEM_HEADS):
        sl = slice(h * head_dim, (h + 1) * head_dim)
        q = q_ref[0, :, sl]
        q = q * lax.rsqrt(jnp.mean(q * q, axis=-1, keepdims=True) + EPS) * g_ref[...]
        k = k_ref[0, :, sl].astype(BF16)
        s = lax.dot_general(q.astype(BF16), k, (((1,), (1,)), ((), ())),
                            preferred_element_type=F32) * scale
        s = s - jnp.max(s, axis=-1, keepdims=True)
        e = jnp.exp(s)
        p = e / jnp.sum(e, axis=-1, keepdims=True)
        o = jnp.dot(p.astype(BF16), v_ref[0, :, sl].astype(BF16), preferred_element_type=F32)
        o_ref[0, :, sl] = o.astype(o_ref.dtype)


def mem_attend(u, col_block, mem_k, mem_v, g_q):
    nb, t, _ = u.shape
    mt, w = mem_k.shape[1:]
    head_dim = w // MEM_HEADS
    tq = min(t, 512)
    assert t % tq == 0
    return pl.pallas_call(
        functools.partial(_mem_attend_body, head_dim=head_dim),
        grid=(nb, t // tq),
        in_specs=[
            pl.BlockSpec((1, tq, w), lambda b, i: (b, i, col_block)),
            pl.BlockSpec((1, mt, w), lambda b, i: (b, 0, 0)),
            pl.BlockSpec((1, mt, w), lambda b, i: (b, 0, 0)),
            pl.BlockSpec((1, head_dim), lambda b, i: (0, 0)),
        ],
        out_specs=pl.BlockSpec((1, tq, w), lambda b, i: (b, i, 0)),
        out_shape=jax.ShapeDtypeStruct((nb, t, w), BF16),
        compiler_params=_params("parallel", "parallel"),
        name="mem_attend",
    )(u, mem_k, mem_v, g_q.reshape(1, head_dim))


def _group_norm_body(x_ref, g_ref, o_ref, *, group):
    for h in range(x_ref.shape[1] // group):
        sl = slice(h * group, (h + 1) * group)
        x = x_ref[:, sl]
        y = x * lax.rsqrt(jnp.mean(x * x, axis=-1, keepdims=True) + EPS)
        o_ref[:, sl] = (y * g_ref[...]).astype(o_ref.dtype)


def group_rmsnorm(x, width, group, g):
    m = x.shape[0]
    tm = min(m, 512)
    assert m % tm == 0
    return pl.pallas_call(
        functools.partial(_group_norm_body, group=group),
        grid=(m // tm,),
        in_specs=[pl.BlockSpec((tm, width), lambda i: (i, 0)),
                  pl.BlockSpec((1, group), lambda i: (0, 0))],
        out_specs=pl.BlockSpec((tm, width), lambda i: (i, 0)),
        out_shape=jax.ShapeDtypeStruct((m, width), F32),
        compiler_params=_params("parallel"),
        name="group_rmsnorm",
    )(x, g.reshape(1, group))


def _norm_rope_body(x_ref, g_ref, cos_ref, sa_ref, sb_ref, *o_refs, flat_out, head_scale):
    half = ROT_DIM // 2
    cos, sa, sb = cos_ref[...], sa_ref[...], sb_ref[...]
    n_sub = x_ref.shape[2] // DIFF_QK_DIM
    for hc in range(n_sub):
        sl = slice(hc * DIFF_QK_DIM, (hc + 1) * DIFF_QK_DIM)
        x = x_ref[0, :, sl]
        y = x * lax.rsqrt(jnp.mean(x * x, axis=-1, keepdims=True) + EPS) * g_ref[...]
        r = (y * cos + pltpu.roll(y, DIFF_QK_DIM - half, axis=1) * sa
             + pltpu.roll(y, half, axis=1) * sb)
        h, c = divmod(hc, 2)
        heads_ref = o_refs[-1]
        heads_ref[0, h, :, c * DIFF_QK_DIM:(c + 1) * DIFF_QK_DIM] = (r * head_scale).astype(heads_ref.dtype)
        if flat_out:
            o_refs[0][0, pl.ds(hc, x_ref.shape[1], stride=n_sub), :] = r


def rope_tables(pos):
    half = ROT_DIM // 2
    inv = jnp.power(ROPE_THETA, -jnp.arange(0, ROT_DIM, 2, dtype=F32) / ROT_DIM)
    ang = pos.astype(F32)[:, None] * inv[None, :]
    cos, sin = jnp.cos(ang), jnp.sin(ang)
    t = pos.shape[0]
    rest = DIFF_QK_DIM - ROT_DIM
    cos_t = jnp.concatenate([cos, cos, jnp.ones((t, rest), F32)], axis=1)
    sa_t = jnp.concatenate([-sin, jnp.zeros((t, half + rest), F32)], axis=1)
    sb_t = jnp.concatenate([jnp.zeros((t, half), F32), sin, jnp.zeros((t, rest), F32)], axis=1)
    return cos_t, sa_t, sb_t


def norm_rope(x, width, g, tables, flat_out, head_scale=1.0):
    nb, t, _ = x.shape
    nh = width // DIFF_V_DIM
    n_sub = width // DIFF_QK_DIM
    tt = min(t, 256)
    assert t % tt == 0
    tab_spec = pl.BlockSpec((tt, DIFF_QK_DIM), lambda b, i: (i, 0))
    out_specs = [pl.BlockSpec((1, nh, tt, DIFF_V_DIM), lambda b, i: (b, 0, i, 0))]
    out_shape = [jax.ShapeDtypeStruct((nb, nh, t, DIFF_V_DIM), BF16)]
    if flat_out:
        out_specs.insert(0, pl.BlockSpec((1, tt * n_sub, DIFF_QK_DIM), lambda b, i: (b, i, 0)))
        out_shape.insert(0, jax.ShapeDtypeStruct((nb, t * n_sub, DIFF_QK_DIM), F32))
    return pl.pallas_call(
        functools.partial(_norm_rope_body, flat_out=flat_out, head_scale=head_scale),
        grid=(nb, t // tt),
        in_specs=[pl.BlockSpec((1, tt, width), lambda b, i: (b, i, 0)),
                  pl.BlockSpec((1, DIFF_QK_DIM), lambda b, i: (0, 0)),
                  tab_spec, tab_spec, tab_spec],
        out_specs=out_specs,
        out_shape=out_shape,
        compiler_params=_params("parallel", "parallel"),
        name="norm_rope",
    )(x, g.reshape(1, DIFF_QK_DIM), *tables)


def _split_heads_body(x_ref, o32_ref, o16_ref):
    for h in range(o32_ref.shape[1]):
        x = x_ref[0, :, h * DIFF_V_DIM:(h + 1) * DIFF_V_DIM]
        o32_ref[0, h] = x
        o16_ref[0, h] = x.astype(o16_ref.dtype)


def split_heads(x, col_block, width):
    nb, t, _ = x.shape
    nh = width // DIFF_V_DIM
    tt = min(t, 256)
    assert t % tt == 0
    spec = pl.BlockSpec((1, nh, tt, DIFF_V_DIM), lambda b, i: (b, 0, i, 0))
    return pl.pallas_call(
        _split_heads_body,
        grid=(nb, t // tt),
        in_specs=[pl.BlockSpec((1, tt, width), lambda b, i: (b, i, col_block))],
        out_specs=[spec, spec],
        out_shape=[jax.ShapeDtypeStruct((nb, nh, t, DIFF_V_DIM), F32),
                   jax.ShapeDtypeStruct((nb, nh, t, DIFF_V_DIM), BF16)],
        compiler_params=_params("parallel", "parallel"),
        name="split_heads",
    )(x)


def _subln(o, g, post_scale):
    y = o * lax.rsqrt(jnp.mean(o * o, axis=-1, keepdims=True) + EPS)
    return y * g * post_scale


def _diff_attn_prompt_body(lam_ref, q_ref, k_ref, v_ref, g_ref, o_ref, m_sc, l_sc, acc_sc,
                           *, tq, post_scale):
    qi = pl.program_id(2)
    m_sc[...] = jnp.full(m_sc.shape, -jnp.inf, F32)
    l_sc[...] = jnp.zeros(l_sc.shape, F32)
    acc_sc[...] = jnp.zeros(acc_sc.shape, F32)

    def step(ki, masked):
        start = pl.multiple_of(ki * tq, tq)
        v = v_ref[0, 0, pl.ds(start, tq), :]
        for c in range(2):
            sl = slice(c * DIFF_QK_DIM, (c + 1) * DIFF_QK_DIM)
            s = lax.dot_general(q_ref[0, 0, :, sl], k_ref[0, 0, pl.ds(start, tq), sl],
                                (((1,), (1,)), ((), ())), preferred_element_type=F32)
            if masked:
                row = lax.broadcasted_iota(jnp.int32, s.shape, 0)
                col = lax.broadcasted_iota(jnp.int32, s.shape, 1)
                s = jnp.where(col <= row, s, -jnp.inf)
            m_prev = m_sc[c]
            m_new = jnp.maximum(m_prev, jnp.max(s, axis=-1, keepdims=True))
            alpha = jnp.exp2(m_prev - m_new)
            p = jnp.exp2(s - jnp.tile(m_new, (1, tq // LANES)))
            l_sc[c] = alpha * l_sc[c] + jnp.sum(p, axis=-1, keepdims=True)
            acc_sc[c] = (jnp.tile(alpha, (1, DIFF_V_DIM // LANES)) * acc_sc[c]
                         + jnp.dot(p.astype(BF16), v, preferred_element_type=F32))
            m_sc[c] = m_new

    def full_step(ki, carry):
        step(ki, False)
        return carry

    lax.fori_loop(0, qi, full_step, 0)
    step(qi, True)
    reps = (1, DIFF_V_DIM // LANES)
    o = (acc_sc[0] / jnp.tile(l_sc[0], reps)
         - lam_ref[0, 0] * (acc_sc[1] / jnp.tile(l_sc[1], reps)))
    o_ref[0] = _subln(o, g_ref[...], post_scale).astype(o_ref.dtype)


def diff_attn_prompt(q, k, v, lam, g_subln, post_scale):
    nb, nh, t, _ = q.shape
    tq = min(t, 512)
    assert t % tq == 0
    body = functools.partial(_diff_attn_prompt_body, tq=tq, post_scale=post_scale)
    kv_spec = pl.BlockSpec((1, 1, t, DIFF_V_DIM), lambda b, h, i: (b, h, 0, 0))
    return pl.pallas_call(
        body,
        grid=(nb, nh, t // tq),
        in_specs=[
            pl.BlockSpec(memory_space=pltpu.SMEM),
            pl.BlockSpec((1, 1, tq, DIFF_V_DIM), lambda b, h, i: (b, h, i, 0)),
            kv_spec, kv_spec,
            pl.BlockSpec((1, DIFF_V_DIM), lambda b, h, i: (0, 0)),
        ],
        out_specs=pl.BlockSpec((1, tq, DIFF_V_DIM), lambda b, h, i: (b, i, h)),
        out_shape=jax.ShapeDtypeStruct((nb, t, nh * DIFF_V_DIM), BF16),
        scratch_shapes=[pltpu.VMEM((2, tq, LANES), F32), pltpu.VMEM((2, tq, LANES), F32),
                        pltpu.VMEM((2, tq, DIFF_V_DIM), F32)],
        compiler_params=_params("parallel", "parallel", "parallel"),
        name="diff_attn_prompt",
    )(lam.reshape(1, 1), q, k, v, g_subln.reshape(1, DIFF_V_DIM))


def _diff_attn_sample_body(pt_ref, lam_ref, qt_ref, *rest, n_steps, pages_per_step, n_heads,
                           post_scale):
    pp = pages_per_step
    k_refs, v_refs = rest[:pp], rest[pp:2 * pp]
    kn_ref, vn_ref, bias_ref, g_ref, o_ref, st_sc, m_sc, l_sc, acc_sc = rest[2 * pp:]
    p_id = pl.program_id(1)
    scale = DIFF_QK_DIM ** -0.5
    n_sub = 2 * n_heads
    pair = 2 * SUBLANES_F32

    @pl.when(p_id == 0)
    def _():
        m_sc[...] = jnp.full(m_sc.shape, -jnp.inf, F32)
        l_sc[...] = jnp.zeros(l_sc.shape, F32)
        acc_sc[...] = jnp.zeros(acc_sc.shape, F32)

    def update(k_pages, v_pages, bias):
        qt = qt_ref[0]
        for r, kp in enumerate(k_pages):
            st_sc[r] = jnp.dot(kp[...].astype(BF16), qt, preferred_element_type=F32)
        col = lax.broadcasted_iota(jnp.int32, (1, SAMPLE_COLS), 1)
        sub_of_col = ((col >> 4) * 2 + ((col >> 2) & 1)) * 2 + ((col >> 3) & 1)
        s_t = []
        for r in range(len(k_pages)):
            s = jnp.zeros((PAGE_SIZE, SAMPLE_COLS), F32)
            for hc in range(n_sub):
                rows = st_sc[r, pl.ds(hc, PAGE_SIZE, stride=n_sub), :]
                s = jnp.where(sub_of_col == hc, rows, s)
            x = (s * scale).T
            s_t.append(x if bias is None else x + bias)
        m_prev = m_sc[...]
        m_new = m_prev
        for x in s_t:
            m_new = jnp.maximum(m_new, jnp.max(x, axis=-1, keepdims=True))
        alpha = jnp.exp(m_prev - m_new)
        l_new = alpha * l_sc[...]
        p16 = []
        for x in s_t:
            p = jnp.exp(x - m_new)
            l_new = l_new + jnp.sum(p, axis=-1, keepdims=True)
            p16.append(p.astype(BF16))
        l_sc[...] = l_new
        m_sc[...] = m_new
        for h in range(n_heads):
            rows = slice((h // 2) * pair, (h // 2 + 1) * pair)
            pv = None
            for p, vp in zip(p16, v_pages):
                d = jnp.dot(p[rows, :], vp[h].astype(BF16), preferred_element_type=F32)
                pv = d if pv is None else pv + d
            acc_sc[h] = alpha[rows] * acc_sc[h] + pv

    @pl.when(p_id < n_steps)
    def _():
        update([r.at[0] for r in k_refs], [r.at[0] for r in v_refs], None)

    @pl.when(p_id == n_steps)
    def _():
        update([kn_ref.at[0]], [vn_ref.at[0]], bias_ref[...])
        lam = lam_ref[0, 0]
        for h in range(n_heads):
            r0 = (h // 2) * pair
            l0 = l_sc[r0:r0 + SUBLANES_F32, :]
            l1 = l_sc[r0 + SUBLANES_F32:r0 + pair, :]
            o = (acc_sc[h, 0:SUBLANES_F32, :] / l0
                 - lam * (acc_sc[h, SUBLANES_F32:pair, :] / l1))
            o_ref[0, h] = _subln(o, g_ref[...], post_scale).astype(o_ref.dtype)


def diff_attn_sample(q, k_new, v_new, cache_k, cache_v, page_table, lam, g_subln, post_scale):
    nb, t, nh = q.shape[:3]
    n_sub = 2 * nh
    n_pages = page_table.shape[1]
    pp = SAMPLE_PAGES_PER_STEP if n_pages % SAMPLE_PAGES_PER_STEP == 0 else 1
    n_steps = n_pages // pp
    assert nh % 2 == 0 and 2 * t == SUBLANES_F32 and n_sub * t <= SAMPLE_COLS
    n_pairs = nh // 2
    qt = q.reshape(nb, t, n_pairs, 2, 2, DIFF_QK_DIM).transpose(0, 5, 2, 4, 3, 1)
    qt = qt.reshape(nb, DIFF_QK_DIM, n_sub * t)
    qt = jnp.pad(qt, ((0, 0), (0, 0), (0, SAMPLE_COLS - n_sub * t))).astype(BF16)
    page_rows = PAGE_SIZE * n_sub
    kn = jnp.pad(k_new.reshape(nb, t * n_sub, DIFF_QK_DIM), ((0, 0), (0, page_rows - t * n_sub), (0, 0)))
    vn = jnp.pad(v_new, ((0, 0), (0, 0), (0, PAGE_SIZE - t), (0, 0)))
    col_q = jnp.arange(SAMPLE_COLS) % t
    tok = jnp.arange(PAGE_SIZE)
    bias = jnp.where((tok[None, :] <= col_q[:, None]) & (tok[None, :] < t), 0.0, -jnp.inf).astype(F32)
    ck = cache_k.reshape(cache_k.shape[0], page_rows, DIFF_QK_DIM)
    cv = cache_v.transpose(0, 2, 1, 3)

    def page_index(p, pt, b, r):
        return pt[b, jnp.minimum(p, n_steps - 1) * pp + r]

    def k_spec(r):
        return pl.BlockSpec((1, page_rows, DIFF_QK_DIM),
                            lambda b, p, pt: (page_index(p, pt, b, r), 0, 0))

    def v_spec(r):
        return pl.BlockSpec((1, nh, PAGE_SIZE, DIFF_V_DIM),
                            lambda b, p, pt: (page_index(p, pt, b, r), 0, 0, 0))

    body = functools.partial(_diff_attn_sample_body, n_steps=n_steps, pages_per_step=pp,
                             n_heads=nh, post_scale=post_scale)
    grid_spec = pltpu.PrefetchScalarGridSpec(
        num_scalar_prefetch=1,
        grid=(nb, n_steps + 1),
        in_specs=(
            [pl.BlockSpec(memory_space=pltpu.SMEM),
             pl.BlockSpec((1, DIFF_QK_DIM, SAMPLE_COLS), lambda b, p, pt: (b, 0, 0))]
            + [k_spec(r) for r in range(pp)] + [v_spec(r) for r in range(pp)]
            + [pl.BlockSpec((1, page_rows, DIFF_QK_DIM), lambda b, p, pt: (b, 0, 0)),
               pl.BlockSpec((1, nh, PAGE_SIZE, DIFF_V_DIM), lambda b, p, pt: (b, 0, 0, 0)),
               pl.BlockSpec((SAMPLE_COLS, PAGE_SIZE), lambda b, p, pt: (0, 0)),
               pl.BlockSpec((1, DIFF_V_DIM), lambda b, p, pt: (0, 0))]),
        out_specs=pl.BlockSpec((1, nh, SUBLANES_F32, DIFF_V_DIM), lambda b, p, pt: (b, 0, 0, 0)),
        scratch_shapes=[pltpu.VMEM((pp, page_rows, SAMPLE_COLS), F32),
                        pltpu.VMEM((SAMPLE_COLS, 1), F32), pltpu.VMEM((SAMPLE_COLS, 1), F32),
                        pltpu.VMEM((nh, 2 * SUBLANES_F32, DIFF_V_DIM), F32)],
    )
    out = pl.pallas_call(
        body,
        grid_spec=grid_spec,
        out_shape=jax.ShapeDtypeStruct((nb, nh, SUBLANES_F32, DIFF_V_DIM), BF16),
        compiler_params=_params("parallel", "arbitrary"),
        name="diff_attn_sample",
    )(page_table, lam.reshape(1, 1), qt, *([ck] * pp), *([cv] * pp), kn, vn, bias,
      g_subln.reshape(1, DIFF_V_DIM))
    out = out.reshape(nb, n_pairs, 2, 2, t, DIFF_V_DIM)
    out = jnp.stack([out[:, :, 0, 0], out[:, :, 1, 1]], axis=2)
    return out.transpose(0, 3, 1, 2, 4).reshape(nb, t, nh * DIFF_V_DIM)


def kernel(x_prompt, x_sample, mem_prompt, state_conv, state_ffn, cache_k, cache_v, cache_mem_k, cache_mem_v, page_table, g_mix, g_ffn, w_in_a, w_dw_a, b_dw_a, ln_g_a, ln_b_a, w_out_a, g_kv, w_kv, g_k_diff, w_in_b, g_q_diff, lambda_q1, lambda_k1, lambda_q2, lambda_k2, g_subln, w_out_b, g_mem, w_mem_kv, g_mem_q, g_mem_k, w_ffn_up, w_ffn_dw, b_ffn_dw, w_ffn_down):
    depth = g_mix.shape[0]
    n_a = w_in_a.shape[0]
    d_model = x_prompt.shape[-1]
    conv_dim = w_dw_a.shape[-1]
    mem_width = w_mem_kv.shape[-1] // 2
    mem_head_dim = mem_width // MEM_HEADS
    q_width = w_kv.shape[-1] // 2
    n_heads = q_width // DIFF_V_DIM

    cast = lambda w: w.astype(BF16)
    w_in_a_b, w_out_a_b, w_in_b_b, w_out_b_b = cast(w_in_a), cast(w_out_a), cast(w_in_b), cast(w_out_b)
    w_kv_b, w_mem_kv_b, w_ffn_down_b = cast(w_kv.reshape(1, *w_kv.shape)), cast(w_mem_kv), cast(w_ffn_down)

    def ffn(x2, l, ffn_prev, seq_len, t_valid):
        h = rmsnorm_cast(x2, g_ffn[l])
        g, st = ffn_up(h, w_ffn_up, w_ffn_dw, b_ffn_dw, l, ffn_prev, seq_len, t_valid)
        return matmul(g, w_ffn_down_b, l, res=x2), st

    def trunk(x, t_valid, pos, mem_k, mem_v, conv_prev, ffn_prev, attend):
        nb, t = x.shape[:2]
        m = nb * t
        x2 = x.reshape(m, d_model)
        tables = rope_tables(pos)
        conv_new, ffn_new = [], []
        k = v = None
        for l in range(depth):
            h = rmsnorm_cast(x2, g_mix[l])
            mk = mem_k[l].reshape(nb, -1, mem_width)
            mv = mem_v[l].reshape(nb, -1, mem_width)
            if l < n_a:
                i = l
                u = matmul(h, w_in_a_b, i).reshape(nb, t, -1)
                c, st = conv_module(u, conv_prev[i], w_dw_a[i], b_dw_a[i], ln_g_a[i], ln_b_a[i],
                                    conv_dim, t_valid)
                conv_new.append(st)
                mo = mem_attend(u, 2 * conv_dim // mem_width, mk, mv, g_mem_q[l])
                cm = jnp.concatenate([c, mo], axis=-1).reshape(m, -1)
                x2 = matmul(cm, w_out_a_b, i, res=x2)
            else:
                j = l - n_a
                lambda_init = 0.8 - 0.6 * math.exp(-0.3 * l)
                u = matmul(h, w_in_b_b, j).reshape(nb, t, -1)
                lam = (jnp.exp(jnp.sum(lambda_q1[j].astype(F32) * lambda_k1[j].astype(F32)))
                       - jnp.exp(jnp.sum(lambda_q2[j].astype(F32) * lambda_k2[j].astype(F32)))
                       + lambda_init)
                o = attend(u, g_q_diff[j], tables, k, v, lam, g_subln[j], 1.0 - lambda_init)
                mo = mem_attend(u, q_width // mem_width, mk, mv, g_mem_q[l])
                om = jnp.concatenate([o, mo], axis=-1).reshape(m, -1)
                x2 = matmul(om, w_out_b_b, j, res=x2)
            x2, st = ffn(x2, l, ffn_prev[l], t, t_valid)
            ffn_new.append(st)
            if l == n_a - 1:
                kv = matmul(rmsnorm_cast(x2, g_kv), w_kv_b, 0).reshape(nb, t, -1)
                k = norm_rope(kv, q_width, g_k_diff, tables, flat_out=True)
                v = split_heads(kv, 1, q_width)
        return x2.reshape(nb, t, d_model), jnp.stack(conv_new), jnp.stack(ffn_new), k[0], v[0]

    bp, sp = x_prompt.shape[:2]
    mem_tokens = mem_prompt.shape[1]
    mem2 = mem_prompt.reshape(bp * mem_tokens, d_model)
    mem_k_l, mem_v_l = [], []
    for l in range(depth):
        kvm = matmul(rmsnorm_cast(mem2, g_mem[l]), w_mem_kv_b, l)
        mem_k_l.append(group_rmsnorm(kvm, mem_width, mem_head_dim, g_mem_k[l]))
        mem_v_l.append(kvm[:, mem_width:])
    mem_shape = (depth, bp, mem_tokens, MEM_HEADS, mem_head_dim)
    mem_k_prompt = jnp.stack(mem_k_l).reshape(mem_shape)
    mem_v_prompt = jnp.stack(mem_v_l).reshape(mem_shape)
    conv0 = jnp.zeros((n_a, bp, CONV_WIDTH - 1, conv_dim), F32)
    ffn0 = jnp.zeros((depth, bp, FFN_CONV_WIDTH - 1, w_ffn_up.shape[-1]), F32)

    def attend_prompt(u, g_q, tables, k, v, lam, g, post_scale):
        (q,) = norm_rope(u, q_width, g_q, tables, flat_out=False,
                         head_scale=DIFF_QK_DIM ** -0.5 * math.log2(math.e))
        return diff_attn_prompt(q, k[1], v[1], lam, g, post_scale)

    y_prompt, conv_prompt, ffn_prompt, k_prompt, v_prompt = trunk(
        x_prompt, sp, jnp.arange(sp), mem_k_prompt, mem_v_prompt, conv0, ffn0, attend_prompt)

    bs, ts = x_sample.shape[:2]
    past_len = page_table.shape[1] * PAGE_SIZE
    pad_t = ((0, 0), (0, SAMPLE_T_PAD - ts), (0, 0))
    xs = jnp.pad(x_sample, pad_t)
    shape5 = (bs, ts, n_heads, 2, DIFF_QK_DIM)

    def attend_sample(u, g_q, tables, k, v, lam, g, post_scale):
        (q,) = norm_rope(u, q_width, g_q, tables, flat_out=False)
        q5 = q[:, :, :ts].astype(F32).reshape(bs, n_heads, ts, 2, DIFF_QK_DIM).transpose(0, 2, 1, 3, 4)
        o = diff_attn_sample(q5, k[0][:, :ts * 2 * n_heads].reshape(shape5), v[0][:, :, :ts],
                             cache_k, cache_v, page_table, lam, g, post_scale)
        return jnp.pad(o, pad_t)

    y_s, conv_sample, ffn_sample, k_s, v_s = trunk(
        xs, ts, past_len + jnp.arange(SAMPLE_T_PAD), cache_mem_k, cache_mem_v, state_conv,
        state_ffn, attend_sample)

    return (y_prompt, y_s[:, :ts], conv_prompt, ffn_prompt,
            k_prompt.reshape(bp, sp, n_heads, 2, DIFF_QK_DIM), v_prompt.transpose(0, 2, 1, 3),
            mem_k_prompt, mem_v_prompt, conv_sample, ffn_sample,
            k_s[:, :ts * 2 * n_heads].reshape(shape5), v_s[:, :, :ts].transpose(0, 2, 1, 3))
```

```python
import functools
import math

import jax
import jax.numpy as jnp
from jax import lax
from jax.experimental import pallas as pl
from jax.experimental.pallas import tpu as pltpu

EPS = 1e-6
ROPE_THETA = 500000.0
MEM_HEADS = 4
CONV_WIDTH = 31
FFN_CONV_WIDTH = 3
DIFF_QK_DIM = 128
DIFF_V_DIM = 256
ROT_DIM = DIFF_QK_DIM // 4
PAGE_SIZE = 128

LANES = 128
SUBLANES_F32 = 8
SUBLANES_BF16 = 16
VMEM_LIMIT_BYTES = 56 * 1024 * 1024
MATMUL_VMEM_BUDGET = 50 * 1024 * 1024

CONV_HALO = 32
CONV_ROW_BLOCK = 64
FFN_HALO = 8
SAMPLE_T_PAD = 16
SAMPLE_PAGES_PER_STEP = 4
SAMPLE_COLS = 128

BF16 = jnp.bfloat16
F32 = jnp.float32


def _params(*sem):
    return pltpu.CompilerParams(dimension_semantics=sem, vmem_limit_bytes=VMEM_LIMIT_BYTES)


def _rmsnorm_cast_body(x_ref, g_ref, o_ref):
    x = x_ref[...]
    y = x * lax.rsqrt(jnp.mean(x * x, axis=-1, keepdims=True) + EPS)
    o_ref[...] = (y * g_ref[...]).astype(o_ref.dtype)


def rmsnorm_cast(x, g):
    m, d = x.shape
    tm = min(m, 256)
    assert m % tm == 0
    return pl.pallas_call(
        _rmsnorm_cast_body,
        grid=(m // tm,),
        in_specs=[pl.BlockSpec((tm, d), lambda i: (i, 0)),
                  pl.BlockSpec((1, d), lambda i: (0, 0))],
        out_specs=pl.BlockSpec((tm, d), lambda i: (i, 0)),
        out_shape=jax.ShapeDtypeStruct((m, d), BF16),
        compiler_params=_params("parallel"),
        name="rmsnorm_cast",
    )(x, g.reshape(1, d))


def _matmul_body(a_ref, w_ref, *rest, has_res):
    acc = jnp.dot(a_ref[...], w_ref[...], preferred_element_type=F32)
    if has_res:
        res_ref, o_ref = rest
        acc = res_ref[...] + acc
    else:
        (o_ref,) = rest
    o_ref[...] = acc.astype(o_ref.dtype)


def _row_tiles(m):
    return [tm for tm in (1024, 512, 256, 128, m) if m % tm == 0 and tm % SUBLANES_BF16 == 0]


def _matmul_tiles(m, k, n, has_res, n_dots=1):
    for tm in _row_tiles(m):
        for tn in (1024, 512, 256, 128):
            if n % tn:
                continue
            est = (2 * tm * k * 2 + 2 * k * tn * 2 + 2 * tm * tn * 4 * (2 if has_res else 1)
                   + (n_dots - 1) * tm * tn * 4)
            if est <= MATMUL_VMEM_BUDGET:
                return tm, tn
    raise ValueError(f"no matmul tiling for {(m, k, n)}")


def matmul(a, w, layer, res=None):
    m, k = a.shape
    n = w.shape[2]
    has_res = res is not None
    tm, tn = _matmul_tiles(m, k, n, has_res)
    in_specs = [pl.BlockSpec((tm, k), lambda i, j: (i, 0)),
                pl.BlockSpec((None, k, tn), lambda i, j: (layer, 0, j))]
    args = [a, w]
    if has_res:
        in_specs.append(pl.BlockSpec((tm, tn), lambda i, j: (i, j)))
        args.append(res)
    return pl.pallas_call(
        functools.partial(_matmul_body, has_res=has_res),
        grid=(m // tm, n // tn),
        in_specs=in_specs,
        out_specs=pl.BlockSpec((tm, tn), lambda i, j: (i, j)),
        out_shape=jax.ShapeDtypeStruct((m, n), F32),
        compiler_params=_params("parallel", "parallel"),
        name="matmul_res" if has_res else "matmul",
    )(*args)


def _matmul_pair_body(a1_ref, a2_ref, w1_ref, w2_ref, res_ref, o_ref):
    acc = jnp.dot(a1_ref[...], w1_ref[...], preferred_element_type=F32)
    acc = acc + jnp.dot(a2_ref[...], w2_ref[...], preferred_element_type=F32)
    o_ref[...] = res_ref[...] + acc


def matmul_pair(a1, a2, w, layer, res):
    m, k1 = a1.shape
    k2 = a2.shape[1]
    n = w.shape[2]
    assert w.shape[1] == k1 + k2 and k1 % k2 == 0
    tm, tn = _matmul_tiles(m, k1 + k2, n, True, n_dots=2)
    return pl.pallas_call(
        _matmul_pair_body,
        grid=(m // tm, n // tn),
        in_specs=[pl.BlockSpec((tm, k1), lambda i, j: (i, 0)),
                  pl.BlockSpec((tm, k2), lambda i, j: (i, 0)),
                  pl.BlockSpec((None, k1, tn), lambda i, j: (layer, 0, j)),
                  pl.BlockSpec((None, k2, tn), lambda i, j: (layer, k1 // k2, j)),
                  pl.BlockSpec((tm, tn), lambda i, j: (i, j))],
        out_specs=pl.BlockSpec((tm, tn), lambda i, j: (i, j)),
        out_shape=jax.ShapeDtypeStruct((m, n), F32),
        compiler_params=_params("parallel", "parallel"),
        name="matmul_pair",
    )(a1, a2, w, w, res)


def _ffn_up_body(a_ref, wg_ref, wu_ref, dwg_ref, dwu_ref, bg_ref, bu_ref, pg_ref, pu_ref,
                 o_ref, sg_ref, su_ref, bufg, bufu, wbg, wbu, *, chunk, n_chunks, seq_per_tile,
                 tiles_per_seq, t_valid_chunk):
    i = pl.program_id(1)

    @pl.when(i == 0)
    def _():
        wbg[...] = wg_ref[...].astype(BF16)
        wbu[...] = wu_ref[...].astype(BF16)

    halves = ((wbg, dwg_ref, bg_ref, bufg, pg_ref, sg_ref), (wbu, dwu_ref, bu_ref, bufu, pu_ref, su_ref))

    def start_sequence(s):
        for _, _, _, buf, p_ref, _ in halves:
            buf[0:FFN_HALO, :] = jnp.zeros((FFN_HALO, buf.shape[1]), F32)
            buf[FFN_HALO - 2:FFN_HALO, :] = p_ref[s]

    if not seq_per_tile:
        pl.when(i % tiles_per_seq == 0)(lambda: start_sequence(0))

    if seq_per_tile:
        u_tile = [jnp.dot(a_ref[...], wb[...], preferred_element_type=F32) for wb, *_ in halves]

    for c in range(n_chunks):
        if seq_per_tile:
            start_sequence(c)
        rows = slice(c * chunk, (c + 1) * chunk)
        ys = []
        for idx, (wb, dw_ref, b_ref, buf, _, s_ref) in enumerate(halves):
            if seq_per_tile:
                u = u_tile[idx][rows]
            else:
                u = jnp.dot(a_ref[rows, :], wb[...], preferred_element_type=F32)
            buf[FFN_HALO:FFN_HALO + chunk, :] = u
            ys.append(dw_ref[0:1, :] * buf[FFN_HALO - 2:FFN_HALO - 2 + chunk, :]
                      + dw_ref[1:2, :] * buf[FFN_HALO - 1:FFN_HALO - 1 + chunk, :]
                      + dw_ref[2:3, :] * u + b_ref[...])
            if seq_per_tile or c == n_chunks - 1:
                s_ref[c if seq_per_tile else 0] = buf[FFN_HALO + t_valid_chunk - 2:FFN_HALO + t_valid_chunk, :]
            if not seq_per_tile:
                buf[0:FFN_HALO, :] = buf[chunk:chunk + FFN_HALO, :]
        yg, yu = ys
        o_ref[rows, :] = (yg * jax.nn.sigmoid(yg) * yu).astype(o_ref.dtype)


FFN_ROW_TILE = 1024
FFN_ROW_CHUNK = 128


def ffn_up(a, w_up, w_dw, b_dw, layer, prev, seq_len, t_valid):
    m, k = a.shape
    f2 = w_up.shape[2]
    f = f2 // 2
    tn = 256
    assert f % tn == 0
    nj = f // tn
    nb = m // seq_len
    if seq_len >= FFN_ROW_TILE:
        tm, chunk, seq_per_tile = FFN_ROW_TILE, FFN_ROW_CHUNK, 0
        assert seq_len % tm == 0 and t_valid == seq_len
        tiles_per_seq = seq_len // tm
        t_valid_chunk = chunk
    else:
        seq_per_tile = min(nb, FFN_ROW_TILE // seq_len)
        assert nb % seq_per_tile == 0 and seq_len % SUBLANES_BF16 == 0
        tm, chunk, tiles_per_seq = seq_per_tile * seq_len, seq_len, 1
        t_valid_chunk = t_valid
    assert 2 <= t_valid_chunk <= chunk
    seq_block = max(seq_per_tile, 1)
    seq_of_tile = (lambda i: i) if seq_per_tile else (lambda i: i // tiles_per_seq)
    b3 = b_dw.reshape(b_dw.shape[0], 1, f2)
    body = functools.partial(_ffn_up_body, chunk=chunk, n_chunks=tm // chunk, seq_per_tile=seq_per_tile,
                             tiles_per_seq=tiles_per_seq, t_valid_chunk=t_valid_chunk)
    g, sg, su = pl.pallas_call(
        body,
        grid=(nj, m // tm),
        in_specs=[
            pl.BlockSpec((tm, k), lambda j, i: (i, 0)),
            pl.BlockSpec((None, k, tn), lambda j, i: (layer, 0, j)),
            pl.BlockSpec((None, k, tn), lambda j, i: (layer, 0, j + nj)),
            pl.BlockSpec((None, FFN_CONV_WIDTH, tn), lambda j, i: (layer, 0, j)),
            pl.BlockSpec((None, FFN_CONV_WIDTH, tn), lambda j, i: (layer, 0, j + nj)),
            pl.BlockSpec((None, 1, tn), lambda j, i: (layer, 0, j)),
            pl.BlockSpec((None, 1, tn), lambda j, i: (layer, 0, j + nj)),
            pl.BlockSpec((seq_block, 2, tn), lambda j, i: (seq_of_tile(i), 0, j)),
            pl.BlockSpec((seq_block, 2, tn), lambda j, i: (seq_of_tile(i), 0, j + nj)),
        ],
        out_specs=[
            pl.BlockSpec((tm, tn), lambda j, i: (i, j)),
            pl.BlockSpec((seq_block, 2, tn), lambda j, i: (seq_of_tile(i), 0, j)),
            pl.BlockSpec((seq_block, 2, tn), lambda j, i: (seq_of_tile(i), 0, j)),
        ],
        out_shape=[jax.ShapeDtypeStruct((m, f), BF16),
                   jax.ShapeDtypeStruct((nb, 2, f), F32),
                   jax.ShapeDtypeStruct((nb, 2, f), F32)],
        scratch_shapes=[pltpu.VMEM((chunk + FFN_HALO, tn), F32),
                        pltpu.VMEM((chunk + FFN_HALO, tn), F32),
                        pltpu.VMEM((k, tn), BF16),
                        pltpu.VMEM((k, tn), BF16)],
        compiler_params=_params("parallel", "arbitrary"),
        name="ffn_up",
    )(a, w_up, w_up, w_dw, w_dw, b3, b3, prev, prev)
    return g, jnp.concatenate([sg, su], axis=-1)


def _conv_module_body(a_ref, gate_ref, prev_ref, w_ref, b_ref, lg_ref, lb_ref,
                      o_ref, st_ref, buf, cbuf, wrep, *, tt, tiles_per_seq, t_valid_last):
    i = pl.program_id(1)

    @pl.when(i % tiles_per_seq == 0)
    def _():
        buf[0:CONV_HALO, :] = prev_ref[0]

    glu = a_ref[0] * jax.nn.sigmoid(gate_ref[0])
    buf[CONV_HALO:CONV_HALO + tt, :] = glu
    base = CONV_HALO - (CONV_WIDTH - 1)
    rb = min(tt, CONV_ROW_BLOCK)
    sub = SUBLANES_F32

    @pl.when(i == 0)
    def _():
        for j in range(CONV_WIDTH):
            wrep[j * sub:(j + 1) * sub, :] = jnp.broadcast_to(w_ref[j:j + 1, :], (sub, wrep.shape[1]))
        wrep[CONV_WIDTH * sub:(CONV_WIDTH + 1) * sub, :] = jnp.broadcast_to(b_ref[...], (sub, wrep.shape[1]))

    def row_block(r, carry):
        r0 = pl.multiple_of(r * rb, rb)
        for l0 in range(0, buf.shape[1], LANES):
            lanes = slice(l0, l0 + LANES)
            xa = buf[pl.ds(r0, rb + CONV_HALO), lanes]
            acc = [wrep[CONV_WIDTH * sub:(CONV_WIDTH + 1) * sub, lanes]] * (rb // sub)
            for b in range(sub):
                taps = list(range(b, CONV_WIDTH, sub))
                sb = xa[base + b:base + b + rb + sub * (len(taps) - 1)]
                for a, j in enumerate(taps):
                    w8 = wrep[j * sub:(j + 1) * sub, lanes]
                    acc = [acc[v] + w8 * sb[sub * (a + v):sub * (a + v + 1)] for v in range(rb // sub)]
            cbuf[pl.ds(r0, rb), lanes] = jnp.concatenate(acc, axis=0)
        return carry

    lax.fori_loop(0, tt // rb, row_block, 0)
    c = cbuf[...]
    mu = jnp.mean(c, axis=-1, keepdims=True)
    d = c - mu
    var = jnp.mean(d * d, axis=-1, keepdims=True)
    y = d * lax.rsqrt(var + EPS) * lg_ref[...] + lb_ref[...]
    o_ref[0] = (y * jax.nn.sigmoid(y)).astype(o_ref.dtype)
    st_ref[0] = buf[base + t_valid_last:base + t_valid_last + CONV_WIDTH - 1, :]
    buf[0:CONV_HALO, :] = buf[tt:tt + CONV_HALO, :]


def conv_module(u, prev, w_dw, b_dw, ln_g, ln_b, c_dim, t_valid):
    nb, t, _ = u.shape
    tt = min(t, 256)
    assert t % tt == 0
    tiles_per_seq = t // tt
    t_valid_last = t_valid - (tiles_per_seq - 1) * tt
    assert 0 < t_valid_last <= tt
    prev_pad = jnp.pad(prev, ((0, 0), (CONV_HALO - (CONV_WIDTH - 1), 0), (0, 0)))
    body = functools.partial(_conv_module_body, tt=tt, tiles_per_seq=tiles_per_seq,
                             t_valid_last=t_valid_last)
    row = lambda v: v.reshape(1, c_dim)
    return pl.pallas_call(
        body,
        grid=(nb, tiles_per_seq),
        in_specs=[
            pl.BlockSpec((1, tt, c_dim), lambda b, i: (b, i, 0)),
            pl.BlockSpec((1, tt, c_dim), lambda b, i: (b, i, 1)),
            pl.BlockSpec((1, CONV_HALO, c_dim), lambda b, i: (b, 0, 0)),
            pl.BlockSpec((CONV_WIDTH, c_dim), lambda b, i: (0, 0)),
            pl.BlockSpec((1, c_dim), lambda b, i: (0, 0)),
            pl.BlockSpec((1, c_dim), lambda b, i: (0, 0)),
            pl.BlockSpec((1, c_dim), lambda b, i: (0, 0)),
        ],
        out_specs=[
            pl.BlockSpec((1, tt, c_dim), lambda b, i: (b, i, 0)),
            pl.BlockSpec((1, CONV_WIDTH - 1, c_dim), lambda b, i: (b, 0, 0)),
        ],
        out_shape=[jax.ShapeDtypeStruct((nb, t, c_dim), BF16),
                   jax.ShapeDtypeStruct((nb, CONV_WIDTH - 1, c_dim), F32)],
        scratch_shapes=[pltpu.VMEM((tt + CONV_HALO, c_dim), F32), pltpu.VMEM((tt, c_dim), F32),
                        pltpu.VMEM(((CONV_WIDTH + 1) * SUBLANES_F32, c_dim), F32)],
        compiler_params=_params("parallel", "arbitrary"),
        name="conv_module",
    )(u, u, prev_pad, w_dw, row(b_dw), row(ln_g), row(ln_b))


def _mem_attend_body(q_ref, k_ref, v_ref, g_ref, o_ref, *, head_dim):
    scale = head_dim ** -0.5
    for h in range(MEM_HEADS):
        sl = slice(h * head_dim, (h + 1) * head_dim)
        q = q_ref[0, :, sl]
        q = q * lax.rsqrt(jnp.mean(q * q, axis=-1, keepdims=True) + EPS) * g_ref[...]
        k = k_ref[0, :, sl].astype(BF16)
        s = lax.dot_general(q.astype(BF16), k, (((1,), (1,)), ((), ())),
                            preferred_element_type=F32) * scale
        s = s - jnp.max(s, axis=-1, keepdims=True)
        e = jnp.exp(s)
        p = e / jnp.sum(e, axis=-1, keepdims=True)
        o = jnp.dot(p.astype(BF16), v_ref[0, :, sl].astype(BF16), preferred_element_type=F32)
        o_ref[0, :, sl] = o.astype(o_ref.dtype)


def mem_attend(u, col_block, mem_k, mem_v, g_q):
    nb, t, _ = u.shape
    mt, w = mem_k.shape[1:]
    head_dim = w // MEM_HEADS
    tq = min(t, 512)
    assert t % tq == 0
    return pl.pallas_call(
        functools.partial(_mem_attend_body, head_dim=head_dim),
        grid=(nb, t // tq),
        in_specs=[
            pl.BlockSpec((1, tq, w), lambda b, i: (b, i, col_block)),
            pl.BlockSpec((1, mt, w), lambda b, i: (b, 0, 0)),
            pl.BlockSpec((1, mt, w), lambda b, i: (b, 0, 0)),
            pl.BlockSpec((1, head_dim), lambda b, i: (0, 0)),
        ],
        out_specs=pl.BlockSpec((1, tq, w), lambda b, i: (b, i, 0)),
        out_shape=jax.ShapeDtypeStruct((nb, t, w), BF16),
        compiler_params=_params("parallel", "parallel"),
        name="mem_attend",
    )(u, mem_k, mem_v, g_q.reshape(1, head_dim))


def _group_norm_body(x_ref, g_ref, o_ref, *, group):
    for h in range(x_ref.shape[1] // group):
        sl = slice(h * group, (h + 1) * group)
        x = x_ref[:, sl]
        y = x * lax.rsqrt(jnp.mean(x * x, axis=-1, keepdims=True) + EPS)
        o_ref[:, sl] = (y * g_ref[...]).astype(o_ref.dtype)


def group_rmsnorm(x, width, group, g):
    m = x.shape[0]
    tm = min(m, 512)
    assert m % tm == 0
    return pl.pallas_call(
        functools.partial(_group_norm_body, group=group),
        grid=(m // tm,),
        in_specs=[pl.BlockSpec((tm, width), lambda i: (i, 0)),
                  pl.BlockSpec((1, group), lambda i: (0, 0))],
        out_specs=pl.BlockSpec((tm, width), lambda i: (i, 0)),
        out_shape=jax.ShapeDtypeStruct((m, width), F32),
        compiler_params=_params("parallel"),
        name="group_rmsnorm",
    )(x, g.reshape(1, group))


def _norm_rope_body(x_ref, g_ref, cos_ref, sa_ref, sb_ref, *o_refs, flat_out, head_scale):
    half = ROT_DIM // 2
    cos, sa, sb = cos_ref[...], sa_ref[...], sb_ref[...]
    n_sub = x_ref.shape[2] // DIFF_QK_DIM
    for hc in range(n_sub):
        sl = slice(hc * DIFF_QK_DIM, (hc + 1) * DIFF_QK_DIM)
        x = x_ref[0, :, sl]
        y = x * lax.rsqrt(jnp.mean(x * x, axis=-1, keepdims=True) + EPS) * g_ref[...]
        r = (y * cos + pltpu.roll(y, DIFF_QK_DIM - half, axis=1) * sa
             + pltpu.roll(y, half, axis=1) * sb)
        h, c = divmod(hc, 2)
        heads_ref = o_refs[-1]
        heads_ref[0, h, :, c * DIFF_QK_DIM:(c + 1) * DIFF_QK_DIM] = (r * head_scale).astype(heads_ref.dtype)
        if flat_out:
            o_refs[0][0, pl.ds(hc, x_ref.shape[1], stride=n_sub), :] = r


def rope_tables(pos):
    half = ROT_DIM // 2
    inv = jnp.power(ROPE_THETA, -jnp.arange(0, ROT_DIM, 2, dtype=F32) / ROT_DIM)
    ang = pos.astype(F32)[:, None] * inv[None, :]
    cos, sin = jnp.cos(ang), jnp.sin(ang)
    t = pos.shape[0]
    rest = DIFF_QK_DIM - ROT_DIM
    cos_t = jnp.concatenate([cos, cos, jnp.ones((t, rest), F32)], axis=1)
    sa_t = jnp.concatenate([-sin, jnp.zeros((t, half + rest), F32)], axis=1)
    sb_t = jnp.concatenate([jnp.zeros((t, half), F32), sin, jnp.zeros((t, rest), F32)], axis=1)
    return cos_t, sa_t, sb_t


def norm_rope(x, width, g, tables, flat_out, head_scale=1.0):
    nb, t, _ = x.shape
    nh = width // DIFF_V_DIM
    n_sub = width // DIFF_QK_DIM
    tt = min(t, 256)
    assert t % tt == 0
    tab_spec = pl.BlockSpec((tt, DIFF_QK_DIM), lambda b, i: (i, 0))
    out_specs = [pl.BlockSpec((1, nh, tt, DIFF_V_DIM), lambda b, i: (b, 0, i, 0))]
    out_shape = [jax.ShapeDtypeStruct((nb, nh, t, DIFF_V_DIM), BF16)]
    if flat_out:
        out_specs.insert(0, pl.BlockSpec((1, tt * n_sub, DIFF_QK_DIM), lambda b, i: (b, i, 0)))
        out_shape.insert(0, jax.ShapeDtypeStruct((nb, t * n_sub, DIFF_QK_DIM), F32))
    return pl.pallas_call(
        functools.partial(_norm_rope_body, flat_out=flat_out, head_scale=head_scale),
        grid=(nb, t // tt),
        in_specs=[pl.BlockSpec((1, tt, width), lambda b, i: (b, i, 0)),
                  pl.BlockSpec((1, DIFF_QK_DIM), lambda b, i: (0, 0)),
                  tab_spec, tab_spec, tab_spec],
        out_specs=out_specs,
        out_shape=out_shape,
        compiler_params=_params("parallel", "parallel"),
        name="norm_rope",
    )(x, g.reshape(1, DIFF_QK_DIM), *tables)


def _split_heads_body(x_ref, o32_ref, o16_ref):
    for h in range(o32_ref.shape[1]):
        x = x_ref[0, :, h * DIFF_V_DIM:(h + 1) * DIFF_V_DIM]
        o32_ref[0, h] = x
        o16_ref[0, h] = x.astype(o16_ref.dtype)


def split_heads(x, col_block, width):
    nb, t, _ = x.shape
    nh = width // DIFF_V_DIM
    tt = min(t, 256)
    assert t % tt == 0
    spec = pl.BlockSpec((1, nh, tt, DIFF_V_DIM), lambda b, i: (b, 0, i, 0))
    return pl.pallas_call(
        _split_heads_body,
        grid=(nb, t // tt),
        in_specs=[pl.BlockSpec((1, tt, width), lambda b, i: (b, i, col_block))],
        out_specs=[spec, spec],
        out_shape=[jax.ShapeDtypeStruct((nb, nh, t, DIFF_V_DIM), F32),
                   jax.ShapeDtypeStruct((nb, nh, t, DIFF_V_DIM), BF16)],
        compiler_params=_params("parallel", "parallel"),
        name="split_heads",
    )(x)


def _subln(o, g, post_scale):
    y = o * lax.rsqrt(jnp.mean(o * o, axis=-1, keepdims=True) + EPS)
    return y * g * post_scale


def _diff_attn_prompt_body(lam_ref, q_ref, k_ref, v_ref, g_ref, o_ref, m_sc, l_sc, acc_sc,
                           *, tq, post_scale):
    qi = pl.program_id(2)
    m_sc[...] = jnp.full(m_sc.shape, -jnp.inf, F32)
    l_sc[...] = jnp.zeros(l_sc.shape, F32)
    acc_sc[...] = jnp.zeros(acc_sc.shape, F32)

    def step(ki, masked):
        start = pl.multiple_of(ki * tq, tq)
        v = v_ref[0, 0, pl.ds(start, tq), :]
        for c in range(2):
            sl = slice(c * DIFF_QK_DIM, (c + 1) * DIFF_QK_DIM)
            s = lax.dot_general(q_ref[0, 0, :, sl], k_ref[0, 0, pl.ds(start, tq), sl],
                                (((1,), (1,)), ((), ())), preferred_element_type=F32)
            if masked:
                row = lax.broadcasted_iota(jnp.int32, s.shape, 0)
                col = lax.broadcasted_iota(jnp.int32, s.shape, 1)
                s = jnp.where(col <= row, s, -jnp.inf)
            m_prev = m_sc[c]
            m_new = jnp.maximum(m_prev, jnp.max(s, axis=-1, keepdims=True))
            alpha = jnp.exp2(m_prev - m_new)
            p = jnp.exp2(s - jnp.tile(m_new, (1, tq // LANES)))
            l_sc[c] = alpha * l_sc[c] + jnp.sum(p, axis=-1, keepdims=True)
            acc_sc[c] = (jnp.tile(alpha, (1, DIFF_V_DIM // LANES)) * acc_sc[c]
                         + jnp.dot(p.astype(BF16), v, preferred_element_type=F32))
            m_sc[c] = m_new

    def full_step(ki, carry):
        step(ki, False)
        return carry

    lax.fori_loop(0, qi, full_step, 0)
    step(qi, True)
    reps = (1, DIFF_V_DIM // LANES)
    o = (acc_sc[0] / jnp.tile(l_sc[0], reps)
         - lam_ref[0, 0] * (acc_sc[1] / jnp.tile(l_sc[1], reps)))
    o_ref[0] = _subln(o, g_ref[...], post_scale).astype(o_ref.dtype)


def diff_attn_prompt(q, k, v, lam, g_subln, post_scale):
    nb, nh, t, _ = q.shape
    tq = min(t, 512)
    assert t % tq == 0
    body = functools.partial(_diff_attn_prompt_body, tq=tq, post_scale=post_scale)
    kv_spec = pl.BlockSpec((1, 1, t, DIFF_V_DIM), lambda b, h, i: (b, h, 0, 0))
    return pl.pallas_call(
        body,
        grid=(nb, nh, t // tq),
        in_specs=[
            pl.BlockSpec(memory_space=pltpu.SMEM),
            pl.BlockSpec((1, 1, tq, DIFF_V_DIM), lambda b, h, i: (b, h, i, 0)),
            kv_spec, kv_spec,
            pl.BlockSpec((1, DIFF_V_DIM), lambda b, h, i: (0, 0)),
        ],
        out_specs=pl.BlockSpec((1, tq, DIFF_V_DIM), lambda b, h, i: (b, i, h)),
        out_shape=jax.ShapeDtypeStruct((nb, t, nh * DIFF_V_DIM), BF16),
        scratch_shapes=[pltpu.VMEM((2, tq, LANES), F32), pltpu.VMEM((2, tq, LANES), F32),
                        pltpu.VMEM((2, tq, DIFF_V_DIM), F32)],
        compiler_params=_params("parallel", "parallel", "parallel"),
        name="diff_attn_prompt",
    )(lam.reshape(1, 1), q, k, v, g_subln.reshape(1, DIFF_V_DIM))


def _diff_attn_sample_body(pt_ref, lam_ref, qt_ref, *rest, n_steps, pages_per_step, n_heads,
                           post_scale):
    pp = pages_per_step
    k_refs, v_refs = rest[:pp], rest[pp:2 * pp]
    kn_ref, vn_ref, bias_ref, g_ref, o_ref, st_sc, m_sc, l_sc, acc_sc = rest[2 * pp:]
    p_id = pl.program_id(1)
    scale = DIFF_QK_DIM ** -0.5
    n_sub = 2 * n_heads
    pair = 2 * SUBLANES_F32

    @pl.when(p_id == 0)
    def _():
        m_sc[...] = jnp.full(m_sc.shape, -jnp.inf, F32)
        l_sc[...] = jnp.zeros(l_sc.shape, F32)
        acc_sc[...] = jnp.zeros(acc_sc.shape, F32)

    def update(k_pages, v_pages, bias):
        qt = qt_ref[0]
        for r, kp in enumerate(k_pages):
            st_sc[r] = jnp.dot(kp[...].astype(BF16), qt, preferred_element_type=F32)
        col = lax.broadcasted_iota(jnp.int32, (1, SAMPLE_COLS), 1)
        sub_of_col = ((col >> 4) * 2 + ((col >> 2) & 1)) * 2 + ((col >> 3) & 1)
        s_t = []
        for r in range(len(k_pages)):
            s = jnp.zeros((PAGE_SIZE, SAMPLE_COLS), F32)
            for hc in range(n_sub):
                rows = st_sc[r, pl.ds(hc, PAGE_SIZE, stride=n_sub), :]
                s = jnp.where(sub_of_col == hc, rows, s)
            x = (s * scale).T
            s_t.append(x if bias is None else x + bias)
        m_prev = m_sc[...]
        m_new = m_prev
        for x in s_t:
            m_new = jnp.maximum(m_new, jnp.max(x, axis=-1, keepdims=True))
        alpha = jnp.exp(m_prev - m_new)
        l_new = alpha * l_sc[...]
        p16 = []
        for x in s_t:
            p = jnp.exp(x - m_new)
            l_new = l_new + jnp.sum(p, axis=-1, keepdims=True)
            p16.append(p.astype(BF16))
        l_sc[...] = l_new
        m_sc[...] = m_new
        for h in range(n_heads):
            rows = slice((h // 2) * pair, (h // 2 + 1) * pair)
            pv = None
            for p, vp in zip(p16, v_pages):
                d = jnp.dot(p[rows, :], vp[h].astype(BF16), preferred_element_type=F32)
                pv = d if pv is None else pv + d
            acc_sc[h] = alpha[rows] * acc_sc[h] + pv

    @pl.when(p_id < n_steps)
    def _():
        update([r.at[0] for r in k_refs], [r.at[0] for r in v_refs], None)

    @pl.when(p_id == n_steps)
    def _():
        update([kn_ref.at[0]], [vn_ref.at[0]], bias_ref[...])
        lam = lam_ref[0, 0]
        for h in range(n_heads):
            r0 = (h // 2) * pair
            l0 = l_sc[r0:r0 + SUBLANES_F32, :]
            l1 = l_sc[r0 + SUBLANES_F32:r0 + pair, :]
            o = (acc_sc[h, 0:SUBLANES_F32, :] / l0
                 - lam * (acc_sc[h, SUBLANES_F32:pair, :] / l1))
            o_ref[0, h] = _subln(o, g_ref[...], post_scale).astype(o_ref.dtype)


def diff_attn_sample(q, k_new, v_new, cache_k, cache_v, page_table, lam, g_subln, post_scale):
    nb, t, nh = q.shape[:3]
    n_sub = 2 * nh
    n_pages = page_table.shape[1]
    pp = SAMPLE_PAGES_PER_STEP if n_pages % SAMPLE_PAGES_PER_STEP == 0 else 1
    n_steps = n_pages // pp
    assert nh % 2 == 0 and 2 * t == SUBLANES_F32 and n_sub * t <= SAMPLE_COLS
    n_pairs = nh // 2
    qt = q.reshape(nb, t, n_pairs, 2, 2, DIFF_QK_DIM).transpose(0, 5, 2, 4, 3, 1)
    qt = qt.reshape(nb, DIFF_QK_DIM, n_sub * t)
    qt = jnp.pad(qt, ((0, 0), (0, 0), (0, SAMPLE_COLS - n_sub * t))).astype(BF16)
    page_rows = PAGE_SIZE * n_sub
    kn = jnp.pad(k_new.reshape(nb, t * n_sub, DIFF_QK_DIM), ((0, 0), (0, page_rows - t * n_sub), (0, 0)))
    vn = jnp.pad(v_new, ((0, 0), (0, 0), (0, PAGE_SIZE - t), (0, 0)))
    col_q = jnp.arange(SAMPLE_COLS) % t
    tok = jnp.arange(PAGE_SIZE)
    bias = jnp.where((tok[None, :] <= col_q[:, None]) & (tok[None, :] < t), 0.0, -jnp.inf).astype(F32)
    ck = cache_k.reshape(cache_k.shape[0], page_rows, DIFF_QK_DIM)
    cv = cache_v.transpose(0, 2, 1, 3)

    def page_index(p, pt, b, r):
        return pt[b, jnp.minimum(p, n_steps - 1) * pp + r]

    def k_spec(r):
        return pl.BlockSpec((1, page_rows, DIFF_QK_DIM),
                            lambda b, p, pt: (page_index(p, pt, b, r), 0, 0))

    def v_spec(r):
        return pl.BlockSpec((1, nh, PAGE_SIZE, DIFF_V_DIM),
                            lambda b, p, pt: (page_index(p, pt, b, r), 0, 0, 0))

    body = functools.partial(_diff_attn_sample_body, n_steps=n_steps, pages_per_step=pp,
                             n_heads=nh, post_scale=post_scale)
    grid_spec = pltpu.PrefetchScalarGridSpec(
        num_scalar_prefetch=1,
        grid=(nb, n_steps + 1),
        in_specs=(
            [pl.BlockSpec(memory_space=pltpu.SMEM),
             pl.BlockSpec((1, DIFF_QK_DIM, SAMPLE_COLS), lambda b, p, pt: (b, 0, 0))]
            + [k_spec(r) for r in range(pp)] + [v_spec(r) for r in range(pp)]
            + [pl.BlockSpec((1, page_rows, DIFF_QK_DIM), lambda b, p, pt: (b, 0, 0)),
               pl.BlockSpec((1, nh, PAGE_SIZE, DIFF_V_DIM), lambda b, p, pt: (b, 0, 0, 0)),
               pl.BlockSpec((SAMPLE_COLS, PAGE_SIZE), lambda b, p, pt: (0, 0)),
               pl.BlockSpec((1, DIFF_V_DIM), lambda b, p, pt: (0, 0))]),
        out_specs=pl.BlockSpec((1, nh, SUBLANES_F32, DIFF_V_DIM), lambda b, p, pt: (b, 0, 0, 0)),
        scratch_shapes=[pltpu.VMEM((pp, page_rows, SAMPLE_COLS), F32),
                        pltpu.VMEM((SAMPLE_COLS, 1), F32), pltpu.VMEM((SAMPLE_COLS, 1), F32),
                        pltpu.VMEM((nh, 2 * SUBLANES_F32, DIFF_V_DIM), F32)],
    )
    out = pl.pallas_call(
        body,
        grid_spec=grid_spec,
        out_shape=jax.ShapeDtypeStruct((nb, nh, SUBLANES_F32, DIFF_V_DIM), BF16),
        compiler_params=_params("parallel", "arbitrary"),
        name="diff_attn_sample",
    )(page_table, lam.reshape(1, 1), qt, *([ck] * pp), *([cv] * pp), kn, vn, bias,
      g_subln.reshape(1, DIFF_V_DIM))
    out = out.reshape(nb, n_pairs, 2, 2, t, DIFF_V_DIM)
    out = jnp.stack([out[:, :, 0, 0], out[:, :, 1, 1]], axis=2)
    return out.transpose(0, 3, 1, 2, 4).reshape(nb, t, nh * DIFF_V_DIM)


def kernel(x_prompt, x_sample, mem_prompt, state_conv, state_ffn, cache_k, cache_v, cache_mem_k, cache_mem_v, page_table, g_mix, g_ffn, w_in_a, w_dw_a, b_dw_a, ln_g_a, ln_b_a, w_out_a, g_kv, w_kv, g_k_diff, w_in_b, g_q_diff, lambda_q1, lambda_k1, lambda_q2, lambda_k2, g_subln, w_out_b, g_mem, w_mem_kv, g_mem_q, g_mem_k, w_ffn_up, w_ffn_dw, b_ffn_dw, w_ffn_down):
    depth = g_mix.shape[0]
    n_a = w_in_a.shape[0]
    d_model = x_prompt.shape[-1]
    conv_dim = w_dw_a.shape[-1]
    mem_width = w_mem_kv.shape[-1] // 2
    mem_head_dim = mem_width // MEM_HEADS
    q_width = w_kv.shape[-1] // 2
    n_heads = q_width // DIFF_V_DIM

    cast = lambda w: w.astype(BF16)
    w_in_a_b, w_out_a_b, w_in_b_b, w_out_b_b = cast(w_in_a), cast(w_out_a), cast(w_in_b), cast(w_out_b)
    w_kv_b, w_mem_kv_b, w_ffn_down_b = cast(w_kv.reshape(1, *w_kv.shape)), cast(w_mem_kv), cast(w_ffn_down)

    def ffn(x2, l, ffn_prev, seq_len, t_valid):
        h = rmsnorm_cast(x2, g_ffn[l])
        g, st = ffn_up(h, w_ffn_up, w_ffn_dw, b_ffn_dw, l, ffn_prev, seq_len, t_valid)
        return matmul(g, w_ffn_down_b, l, res=x2), st

    def trunk(x, t_valid, pos, mem_k, mem_v, conv_prev, ffn_prev, attend):
        nb, t = x.shape[:2]
        m = nb * t
        x2 = x.reshape(m, d_model)
        tables = rope_tables(pos)
        conv_new, ffn_new = [], []
        k = v = None
        for l in range(depth):
            h = rmsnorm_cast(x2, g_mix[l])
            mk = mem_k[l].reshape(nb, -1, mem_width)
            mv = mem_v[l].reshape(nb, -1, mem_width)
            if l < n_a:
                i = l
                u = matmul(h, w_in_a_b, i).reshape(nb, t, -1)
                c, st = conv_module(u, conv_prev[i], w_dw_a[i], b_dw_a[i], ln_g_a[i], ln_b_a[i],
                                    conv_dim, t_valid)
                conv_new.append(st)
                mo = mem_attend(u, 2 * conv_dim // mem_width, mk, mv, g_mem_q[l])
                x2 = matmul_pair(c.reshape(m, -1), mo.reshape(m, -1), w_out_a_b, i, x2)
            else:
                j = l - n_a
                lambda_init = 0.8 - 0.6 * math.exp(-0.3 * l)
                u = matmul(h, w_in_b_b, j).reshape(nb, t, -1)
                lam = (jnp.exp(jnp.sum(lambda_q1[j].astype(F32) * lambda_k1[j].astype(F32)))
                       - jnp.exp(jnp.sum(lambda_q2[j].astype(F32) * lambda_k2[j].astype(F32)))
                       + lambda_init)
                o = attend(u, g_q_diff[j], tables, k, v, lam, g_subln[j], 1.0 - lambda_init)
                mo = mem_attend(u, q_width // mem_width, mk, mv, g_mem_q[l])
                x2 = matmul_pair(o.reshape(m, -1), mo.reshape(m, -1), w_out_b_b, j, x2)
            x2, st = ffn(x2, l, ffn_prev[l], t, t_valid)
            ffn_new.append(st)
            if l == n_a - 1:
                kv = matmul(rmsnorm_cast(x2, g_kv), w_kv_b, 0).reshape(nb, t, -1)
                k = norm_rope(kv, q_width, g_k_diff, tables, flat_out=True)
                v = split_heads(kv, 1, q_width)
        return x2.reshape(nb, t, d_model), jnp.stack(conv_new), jnp.stack(ffn_new), k[0], v[0]

    bp, sp = x_prompt.shape[:2]
    mem_tokens = mem_prompt.shape[1]
    mem2 = mem_prompt.reshape(bp * mem_tokens, d_model)
    mem_k_l, mem_v_l = [], []
    for l in range(depth):
        kvm = matmul(rmsnorm_cast(mem2, g_mem[l]), w_mem_kv_b, l)
        mem_k_l.append(group_rmsnorm(kvm, mem_width, mem_head_dim, g_mem_k[l]))
        mem_v_l.append(kvm[:, mem_width:])
    mem_shape = (depth, bp, mem_tokens, MEM_HEADS, mem_head_dim)
    mem_k_prompt = jnp.stack(mem_k_l).reshape(mem_shape)
    mem_v_prompt = jnp.stack(mem_v_l).reshape(mem_shape)
    conv0 = jnp.zeros((n_a, bp, CONV_WIDTH - 1, conv_dim), F32)
    ffn0 = jnp.zeros((depth, bp, FFN_CONV_WIDTH - 1, w_ffn_up.shape[-1]), F32)

    def attend_prompt(u, g_q, tables, k, v, lam, g, post_scale):
        (q,) = norm_rope(u, q_width, g_q, tables, flat_out=False,
                         head_scale=DIFF_QK_DIM ** -0.5 * math.log2(math.e))
        return diff_attn_prompt(q, k[1], v[1], lam, g, post_scale)

    y_prompt, conv_prompt, ffn_prompt, k_prompt, v_prompt = trunk(
        x_prompt, sp, jnp.arange(sp), mem_k_prompt, mem_v_prompt, conv0, ffn0, attend_prompt)

    bs, ts = x_sample.shape[:2]
    past_len = page_table.shape[1] * PAGE_SIZE
    pad_t = ((0, 0), (0, SAMPLE_T_PAD - ts), (0, 0))
    xs = jnp.pad(x_sample, pad_t)
    shape5 = (bs, ts, n_heads, 2, DIFF_QK_DIM)

    def attend_sample(u, g_q, tables, k, v, lam, g, post_scale):
        (q,) = norm_rope(u, q_width, g_q, tables, flat_out=False)
        q5 = q[:, :, :ts].astype(F32).reshape(bs, n_heads, ts, 2, DIFF_QK_DIM).transpose(0, 2, 1, 3, 4)
        o = diff_attn_sample(q5, k[0][:, :ts * 2 * n_heads].reshape(shape5), v[0][:, :, :ts],
                             cache_k, cache_v, page_table, lam, g, post_scale)
        return jnp.pad(o, pad_t)

    y_s, conv_sample, ffn_sample, k_s, v_s = trunk(
        xs, ts, past_len + jnp.arange(SAMPLE_T_PAD), cache_mem_k, cache_mem_v, state_conv,
        state_ffn, attend_sample)

    return (y_prompt, y_s[:, :ts], conv_prompt, ffn_prompt,
            k_prompt.reshape(bp, sp, n_heads, 2, DIFF_QK_DIM), v_prompt.transpose(0, 2, 1, 3),
            mem_k_prompt, mem_v_prompt, conv_sample, ffn_sample,
            k_s[:, :ts * 2 * n_heads].reshape(shape5), v_s[:, :, :ts].transpose(0, 2, 1, 3))
```
